```python
import math
import functools
import jax
import jax.numpy as jnp
from jax import lax
import numpy as np

D_MODEL = 4096
BATCH = 4
SEQ = 2048
DEPTH = 2
DEC_BATCH = 8
DEC_SEQ = 8
PAST_LEN = 16384
PAGE_SIZE = 128

N_A_LAYERS = DEPTH // 2
N_B_LAYERS = DEPTH - N_A_LAYERS
MIX_TOKEN = 3 * D_MODEL // 4
MIX_MEM = D_MODEL // 4
M_HEADS = 6
M_DV = MIX_TOKEN // M_HEADS
M_DQK = M_DV // 2
M_CHUNK = 64
B_HEADS = 24
B_DH = MIX_TOKEN // B_HEADS
MOBA_BLOCK = 256
MOBA_TOPK = 3
MOBA_QCHUNK = 4
MEM_SLOTS = 256
MEM_HEADS = 4
MEM_DH = MIX_MEM // MEM_HEADS
REL_BUCKETS = 32
REL_MAX_DIST = 4096
PEER_HEADS = 8
PEER_NKEYS = 128
PEER_N = PEER_NKEYS * PEER_NKEYS
PEER_DKEY = 128
PEER_TOPK = 16
PEER_TOKCHUNK = 32
DN_ALPHA = (2.0 * DEPTH) ** 0.25
DN_BETA = (8.0 * DEPTH) ** -0.25
LN_EPS = 1e-5
A_SIZES = (M_HEADS * M_DQK, M_HEADS * M_DQK, MIX_TOKEN, MIX_TOKEN, M_HEADS, M_HEADS, MIX_MEM)
A_TOTAL = 2 * M_HEADS * M_DQK + 2 * MIX_TOKEN + 2 * M_HEADS + MIX_MEM

kernel_name = 'yoco_mlstm_moba_peer_step'


def _split(u, sizes):
    cuts, acc = [], 0
    for s in sizes[:-1]:
        acc += s
        cuts.append(acc)
    return jnp.split(u, cuts, axis=-1)


def _layernorm(x, g, b):
    xf = x.astype(jnp.float32)
    mu = xf.mean(-1, keepdims=True)
    var = jnp.square(xf - mu).mean(-1, keepdims=True)
    y = (xf - mu) * lax.rsqrt(var + LN_EPS) * g.astype(jnp.float32) + b.astype(jnp.float32)
    return y.astype(x.dtype)


def _rel_bucket(dist):
    n = jnp.maximum(dist, 0)
    max_exact = REL_BUCKETS // 2
    nf = jnp.maximum(n, 1).astype(jnp.float32)
    large = max_exact + (jnp.log(nf / max_exact) / math.log(REL_MAX_DIST / max_exact)
                         * (REL_BUCKETS - max_exact)).astype(jnp.int32)
    return jnp.where(n < max_exact, n, jnp.minimum(large, REL_BUCKETS - 1))


def _mlstm(q, k, v, i_pre, f_pre, c0, n0, m0):
    B, T = q.shape[:2]
    L = math.gcd(T, M_CHUNK)
    nc = T // L

    def chunks(a):
        a = a.astype(jnp.float32).reshape((B, nc, L) + a.shape[2:])
        return jnp.swapaxes(jnp.moveaxis(a, 1, 0), 2, 3)

    causal = jnp.tril(jnp.ones((L, L), dtype=bool))

    def step(carry, inp):
        c, n, m = carry
        qc, kc, vc, ic, fc = inp
        b = jnp.cumsum(jax.nn.log_sigmoid(fc), axis=-1)
        dmat = jnp.where(causal, b[..., :, None] - b[..., None, :] + ic[..., None, :], -jnp.inf)
        inter = b + m[..., None]
        m_t = jnp.maximum(inter, dmat.max(-1))
        w_inter = jnp.exp(inter - m_t)
        s = jnp.einsum('bhtd,bhsd->bhts', qc, kc) * jnp.exp(dmat - m_t[..., None])
        num = jnp.einsum('bhts,bhsv->bhtv', s, vc) + w_inter[..., None] * jnp.einsum('bhtd,bhdv->bhtv', qc, c)
        den = s.sum(-1) + w_inter * jnp.einsum('bhtd,bhd->bht', qc, n)
        h = num / jnp.maximum(jnp.abs(den), jnp.exp(-m_t))[..., None]
        m_new = m_t[..., -1]
        w_end = jnp.exp(b[..., -1:] - b + ic - m_new[..., None])
        decay = jnp.exp(b[..., -1] + m - m_new)
        c_new = decay[..., None, None] * c + jnp.einsum('bhs,bhsd,bhsv->bhdv', w_end, kc, vc)
        n_new = decay[..., None] * n + jnp.einsum('bhs,bhsd->bhd', w_end, kc)
        return (c_new, n_new, m_new), h

    carry0 = (c0.astype(jnp.float32), n0.astype(jnp.float32), m0.astype(jnp.float32))
    (c, n, m), h = lax.scan(step, carry0, (chunks(q), chunks(k), chunks(v), chunks(i_pre), chunks(f_pre)))
    h = jnp.moveaxis(jnp.swapaxes(h, 2, 3), 0, 1).reshape(B, T, q.shape[2], v.shape[3])
    return h, (c, n, m)


def _mlstm_mixer(x, w_in, b_if, state):
    B, T, _ = x.shape
    q, k, v, o, gi, gf, qm = _split(x @ w_in, A_SIZES)
    q = q.reshape(B, T, M_HEADS, M_DQK)
    k = k.reshape(B, T, M_HEADS, M_DQK) * (M_DQK ** -0.5)
    v = v.reshape(B, T, M_HEADS, M_DV)
    gi = gi.astype(jnp.float32) + b_if[:M_HEADS].astype(jnp.float32)
    gf = gf.astype(jnp.float32) + b_if[M_HEADS:].astype(jnp.float32)
    h, new_state = _mlstm(q, k, v, gi, gf, *state)
    tok = jax.nn.sigmoid(o.astype(jnp.float32)) * h.reshape(B, T, MIX_TOKEN)
    return tok.astype(x.dtype), qm, new_state


def _mem_kv(mem, w):
    B, M, _ = mem.shape
    k, v = _split(mem @ w, (MIX_MEM, MIX_MEM))
    return k.reshape(B, M, MEM_HEADS, MEM_DH), v.reshape(B, M, MEM_HEADS, MEM_DH)


def _mem_attend(qm, mk, mv):
    B, T, _ = qm.shape
    q = qm.reshape(B, T, MEM_HEADS, MEM_DH)
    logits = jnp.einsum('bthd,bmhd->bhtm', q, mk).astype(jnp.float32) * (MEM_DH ** -0.5)
    p = jax.nn.softmax(logits, axis=-1).astype(mv.dtype)
    return jnp.einsum('bhtm,bmhd->bthd', p, mv).reshape(B, T, MIX_MEM)


def _moba_select(q, means, pos):
    gate = jnp.einsum('bthd,bchd->bthc', q.astype(jnp.float32), means)
    j = pos // MOBA_BLOCK
    elig = jnp.arange(means.shape[1])[None, :] < j[:, None]
    gate = jnp.where(elig[None, :, None, :], gate, -jnp.inf)
    _, sel = lax.top_k(gate, MOBA_TOPK)
    valid = jnp.broadcast_to((jnp.arange(MOBA_TOPK)[None, :] < j[:, None])[None, :, None, :], sel.shape)
    return sel, valid


def _moba_core(q, qpos, own_k, own_v, own_start, sel_k, sel_v, sel_blk, sel_valid, rel_bias):
    B, Tc, H, _ = q.shape
    scale = B_DH ** -0.5
    bias_tab = rel_bias.astype(jnp.float32)
    own_pos = own_start + jnp.arange(MOBA_BLOCK)
    d_own = qpos[:, None] - own_pos[None, :]
    lo = (jnp.einsum('bthd,bshd->bths', q, own_k).astype(jnp.float32) * scale
          + jnp.swapaxes(bias_tab[_rel_bucket(d_own)], 1, 2)[None])
    lo = jnp.where((d_own >= 0)[None, :, None, :], lo, -jnp.inf)
    sel_pos = sel_blk[..., None] * MOBA_BLOCK + jnp.arange(MOBA_BLOCK)
    d_sel = qpos[None, :, None, None, None] - sel_pos
    h_idx = jnp.arange(H)[None, None, :, None, None]
    ls = (jnp.einsum('bthd,bthksd->bthks', q, sel_k).astype(jnp.float32) * scale
          + bias_tab[_rel_bucket(d_sel), h_idx])
    ls = jnp.where(sel_valid[..., None], ls, -jnp.inf)
    logits = jnp.concatenate([lo, ls.reshape(B, Tc, H, MOBA_TOPK * MOBA_BLOCK)], axis=-1)
    p = jax.nn.softmax(logits, axis=-1).astype(own_v.dtype)
    p_sel = p[..., MOBA_BLOCK:].reshape(ls.shape)
    return (jnp.einsum('bths,bshd->bthd', p[..., :MOBA_BLOCK], own_v)
            + jnp.einsum('bthks,bthksd->bthd', p_sel, sel_v))


def _to_qchunks(a, n, qc):
    return jnp.swapaxes(a.reshape((a.shape[0], n, qc) + a.shape[2:]), 0, 1)


def _from_qchunks(a):
    a = jnp.swapaxes(a, 0, 1)
    return a.reshape((a.shape[0], a.shape[1] * a.shape[2]) + a.shape[3:])


def _moba_prompt_prep(k, v):
    B, T, H, dh = k.shape
    nb = -(-T // MOBA_BLOCK)
    pad = ((0, 0), (0, nb * MOBA_BLOCK - T), (0, 0), (0, 0))
    kb = jnp.pad(k, pad).reshape(B, nb, MOBA_BLOCK, H, dh)
    vb = jnp.pad(v, pad).reshape(B, nb, MOBA_BLOCK, H, dh)
    nbg = max(nb, MOBA_TOPK)
    means = jnp.pad(kb.astype(jnp.float32).mean(2), ((0, 0), (0, nbg - nb), (0, 0), (0, 0)))
    return (kb, vb, jnp.moveaxis(kb, 3, 1), jnp.moveaxis(vb, 3, 1), means)


def _moba_prompt_attend(q, ctx, rel_bias):
    kb, vb, kbh, vbh, means = ctx
    B, T, H, dh = q.shape
    nb = kb.shape[1]
    sel, valid = _moba_select(q, means, jnp.arange(T, dtype=jnp.int32))
    sel = jnp.minimum(sel, nb - 1)
    qc = math.gcd(T, MOBA_QCHUNK)
    n = T // qc
    b_idx = jnp.arange(B)[:, None, None, None]
    h_idx = jnp.arange(H)[None, None, :, None]

    def body(args):
        c, qx, sx, vx = args
        t0 = c * qc
        j0 = t0 // MOBA_BLOCK
        own_k = lax.dynamic_index_in_dim(kb, j0, axis=1, keepdims=False)
        own_v = lax.dynamic_index_in_dim(vb, j0, axis=1, keepdims=False)
        return _moba_core(qx, t0 + jnp.arange(qc), own_k, own_v, j0 * MOBA_BLOCK,
                          kbh[b_idx, h_idx, sx], vbh[b_idx, h_idx, sx], sx, vx, rel_bias)

    out = lax.map(body, (jnp.arange(n, dtype=jnp.int32), _to_qchunks(q, n, qc),
                         _to_qchunks(sel, n, qc), _to_qchunks(valid, n, qc)))
    return _from_qchunks(out)


def _moba_sample_prep(k_new, v_new, k_pool, v_pool, page_table):
    B, T, H, dh = k_new.shape
    n_pages = page_table.shape[1]
    ppb = MOBA_BLOCK // PAGE_SIZE
    nfull = (n_pages * PAGE_SIZE) // MOBA_BLOCK
    nbg = max(nfull, MOBA_TOPK)
    k_past = k_pool[page_table[:, :nfull * ppb]].reshape(B, nfull, MOBA_BLOCK, H, dh)
    means = jnp.pad(k_past.astype(jnp.float32).mean(2), ((0, 0), (0, nbg - nfull), (0, 0), (0, 0)))
    n_tail = n_pages - nfull * ppb
    tail_pages = page_table[:, nfull * ppb:]
    own_k = jnp.concatenate([k_pool[tail_pages].reshape(B, n_tail * PAGE_SIZE, H, dh), k_new], axis=1)
    own_v = jnp.concatenate([v_pool[tail_pages].reshape(B, n_tail * PAGE_SIZE, H, dh), v_new], axis=1)
    pad = ((0, 0), (0, MOBA_BLOCK - own_k.shape[1]), (0, 0), (0, 0))
    return (means, jnp.pad(own_k, pad), jnp.pad(own_v, pad))


def _moba_sample_attend(q, ctx, k_pool, v_pool, page_table, rel_bias):
    means, own_k, own_v = ctx
    B, T, H, dh = q.shape
    n_pages = page_table.shape[1]
    past = n_pages * PAGE_SIZE
    ppb = MOBA_BLOCK // PAGE_SIZE
    nfull = past // MOBA_BLOCK
    sel, valid = _moba_select(q, means, past + jnp.arange(T, dtype=jnp.int32))
    sel = jnp.minimum(sel, max(nfull - 1, 0))
    b5 = jnp.arange(B)[:, None, None, None, None]
    cols = jnp.minimum(sel[..., None] * ppb + jnp.arange(ppb), n_pages - 1)
    pidx = page_table[b5, cols]
    h5 = jnp.arange(H)[None, None, :, None, None]
    qc = math.gcd(T, MOBA_QCHUNK)
    n = T // qc

    def body(args):
        c, qx, sx, vx, px = args
        sk = k_pool[px, :, h5].reshape(B, qc, H, MOBA_TOPK, MOBA_BLOCK, dh)
        sv = v_pool[px, :, h5].reshape(B, qc, H, MOBA_TOPK, MOBA_BLOCK, dh)
        return _moba_core(qx, past + c * qc + jnp.arange(qc), own_k, own_v, nfull * MOBA_BLOCK,
                          sk, sv, sx, vx, rel_bias)

    out = lax.map(body, (jnp.arange(n, dtype=jnp.int32), _to_qchunks(q, n, qc), _to_qchunks(sel, n, qc),
                         _to_qchunks(valid, n, qc), _to_qchunks(pidx, n, qc)))
    return _from_qchunks(out)


def _peer(x, wq, keys, u_tab, v_tab):
    B, T, D = x.shape
    n_tok = B * T
    xt = x.reshape(n_tok, D)
    q = (xt @ wq).reshape(n_tok, PEER_HEADS, 2, PEER_DKEY // 2)
    s = jnp.einsum('nhpd,hpkd->nhpk', q, keys).astype(jnp.float32)
    s1, i1 = lax.top_k(s[:, :, 0], PEER_TOPK)
    s2, i2 = lax.top_k(s[:, :, 1], PEER_TOPK)
    n_c = PEER_TOPK * PEER_TOPK
    cand = (s1[..., :, None] + s2[..., None, :]).reshape(n_tok, PEER_HEADS, n_c)
    cidx = (i1[..., :, None] * PEER_NKEYS + i2[..., None, :]).reshape(n_tok, PEER_HEADS, n_c)
    top, pos = lax.top_k(cand, PEER_TOPK)
    eidx = jnp.take_along_axis(cidx, pos, axis=-1)
    gate = jax.nn.softmax(top, axis=-1)
    tc = math.gcd(n_tok, PEER_TOKCHUNK)
    nch = n_tok // tc
    n_e = PEER_HEADS * PEER_TOPK

    def body(args):
        xc, ec, gc = args
        a = jnp.einsum('td,ted->te', xc, u_tab[ec]).astype(jnp.float32)
        hc = (jax.nn.gelu(a) * gc).astype(xc.dtype)
        return jnp.einsum('te,ted->td', hc, v_tab[ec])

    out = lax.map(body, (xt.reshape(nch, tc, D), eidx.reshape(nch, tc, n_e), gate.reshape(nch, tc, n_e)))
    return out.reshape(B, T, D)


def _trunk(x, mem_kv, mlstm_init, moba_prep, moba_attend, p):
    B, T, _ = x.shape
    new_states = []
    k_sh = v_sh = ctx = None
    for layer in range(DEPTH):
        if layer < N_A_LAYERS:
            tok, qm, st = _mlstm_mixer(x, p['w_in_a'][layer], p['b_if_a'][layer], mlstm_init[layer])
            new_states.append(st)
        else:
            if ctx is None:
                k_sh, v_sh = _split(x @ p['w_kv_shared'], (MIX_TOKEN, MIX_TOKEN))
                k_sh = k_sh.reshape(B, T, B_HEADS, B_DH)
                v_sh = v_sh.reshape(B, T, B_HEADS, B_DH)
                ctx = moba_prep(k_sh, v_sh)
            qb, qm = _split(x @ p['w_in_b'][layer - N_A_LAYERS], (MIX_TOKEN, MIX_MEM))
            tok = moba_attend(qb.reshape(B, T, B_HEADS, B_DH), ctx).reshape(B, T, MIX_TOKEN)
        mem_out = _mem_attend(qm, *mem_kv[layer])
        mixed = jnp.concatenate([tok, mem_out], axis=-1) @ p['w_out'][layer]
        x = _layernorm(DN_ALPHA * x + mixed, p['ln1_g'][layer], p['ln1_b'][layer])
        ffn = _peer(x, p['peer_wq'][layer], p['peer_keys'][layer], p['peer_u'][layer], p['peer_v'][layer])
        x = _layernorm(DN_ALPHA * x + ffn, p['ln2_g'][layer], p['ln2_b'][layer])
    return x, new_states, k_sh, v_sh


def setup_inputs(seed: int = 0) -> dict:
    key = jax.random.key(seed)
    ks = jax.random.split(key, 27)

    def nrm(i, shape, scale):
        return jax.random.normal(ks[i], shape, jnp.float32) * scale

    n_pages = PAST_LEN // PAGE_SIZE
    n_used = DEC_BATCH * n_pages
    n_pool = n_used + max(1, n_used // 4)
    inv = D_MODEL ** -0.5
    page_table = jax.random.permutation(ks[0], n_pool)[:n_used].reshape(DEC_BATCH, n_pages).astype(jnp.int32)
    return {
        'x_prompt': nrm(1, (BATCH, SEQ, D_MODEL), 1.0),
        'x_sample': nrm(2, (DEC_BATCH, DEC_SEQ, D_MODEL), 1.0),
        'cache_moba_k': nrm(3, (n_pool, PAGE_SIZE, B_HEADS, B_DH), 1.0),
        'cache_moba_v': nrm(4, (n_pool, PAGE_SIZE, B_HEADS, B_DH), 1.0),
        'cache_mem_k': nrm(5, (DEPTH, DEC_BATCH, MEM_SLOTS, MEM_HEADS, MEM_DH), 1.0),
        'cache_mem_v': nrm(6, (DEPTH, DEC_BATCH, MEM_SLOTS, MEM_HEADS, MEM_DH), 1.0),
        'state_mlstm_c': nrm(7, (N_A_LAYERS, DEC_BATCH, M_HEADS, M_DQK, M_DV), 1.0),
        'state_mlstm_n': nrm(8, (N_A_LAYERS, DEC_BATCH, M_HEADS, M_DQK), 1.0),
        'state_mlstm_m': nrm(9, (N_A_LAYERS, DEC_BATCH, M_HEADS), 1.0),
        'page_table': page_table,
        'mem_prompt': nrm(10, (BATCH, MEM_SLOTS, D_MODEL), 1.0),
        'w_in_a': nrm(11, (N_A_LAYERS, D_MODEL, A_TOTAL), inv),
        'b_if_a': jnp.concatenate([nrm(12, (N_A_LAYERS, M_HEADS), 0.1),
                                   3.0 + nrm(13, (N_A_LAYERS, M_HEADS), 0.5)], axis=-1),
        'w_in_b': nrm(14, (N_B_LAYERS, D_MODEL, MIX_TOKEN + MIX_MEM), inv),
        'w_kv_shared': nrm(15, (D_MODEL, 2 * MIX_TOKEN), inv),
        'rel_bias': nrm(16, (REL_BUCKETS, B_HEADS), 0.5),
        'w_mem_kv': nrm(17, (DEPTH, D_MODEL, 2 * MIX_MEM), inv),
        'w_out': nrm(18, (DEPTH, D_MODEL, D_MODEL), inv * DN_BETA),
        'ln1_g': 1.0 + nrm(19, (DEPTH, D_MODEL), 0.02),
        'ln1_b': nrm(20, (DEPTH, D_MODEL), 0.02),
        'ln2_g': 1.0 + nrm(21, (DEPTH, D_MODEL), 0.02),
        'ln2_b': nrm(22, (DEPTH, D_MODEL), 0.02),
        'peer_wq': nrm(23, (DEPTH, D_MODEL, PEER_HEADS * PEER_DKEY), inv),
        'peer_keys': nrm(24, (DEPTH, PEER_HEADS, 2, PEER_NKEYS, PEER_DKEY // 2), (PEER_DKEY // 2) ** -0.5),
        'peer_u': nrm(25, (DEPTH, PEER_N, D_MODEL), inv),
        'peer_v': nrm(26, (DEPTH, PEER_N, D_MODEL), DN_BETA * PEER_HEADS ** -0.5),
    }


def reference(x_prompt, x_sample, cache_moba_k, cache_moba_v, cache_mem_k, cache_mem_v,
              state_mlstm_c, state_mlstm_n, state_mlstm_m, page_table, mem_prompt,
              w_in_a, b_if_a, w_in_b, w_kv_shared, rel_bias, w_mem_kv, w_out,
              ln1_g, ln1_b, ln2_g, ln2_b, peer_wq, peer_keys, peer_u, peer_v):
    params = {'w_in_a': w_in_a, 'b_if_a': b_if_a, 'w_in_b': w_in_b, 'w_kv_shared': w_kv_shared,
              'w_out': w_out, 'ln1_g': ln1_g, 'ln1_b': ln1_b, 'ln2_g': ln2_g, 'ln2_b': ln2_b,
              'peer_wq': peer_wq, 'peer_keys': peer_keys, 'peer_u': peer_u, 'peer_v': peer_v}
    bp = x_prompt.shape[0]
    mem_kv_p = [_mem_kv(mem_prompt, w_mem_kv[l]) for l in range(DEPTH)]
    init_p = [(jnp.zeros((bp, M_HEADS, M_DQK, M_DV), jnp.float32), jnp.zeros((bp, M_HEADS, M_DQK), jnp.float32),
               jnp.zeros((bp, M_HEADS), jnp.float32)) for _ in range(N_A_LAYERS)]
    y_prompt, st_p, k_p, v_p = _trunk(
        x_prompt, mem_kv_p, init_p, _moba_prompt_prep,
        functools.partial(_moba_prompt_attend, rel_bias=rel_bias), params)
    mem_kv_s = [(cache_mem_k[l], cache_mem_v[l]) for l in range(DEPTH)]
    init_s = [(state_mlstm_c[a], state_mlstm_n[a], state_mlstm_m[a]) for a in range(N_A_LAYERS)]
    y_sample, st_s, k_s, v_s = _trunk(
        x_sample, mem_kv_s, init_s,
        functools.partial(_moba_sample_prep, k_pool=cache_moba_k, v_pool=cache_moba_v, page_table=page_table),
        functools.partial(_moba_sample_attend, k_pool=cache_moba_k, v_pool=cache_moba_v,
                          page_table=page_table, rel_bias=rel_bias), params)
    mlstm_c_p = jnp.stack([s[0] for s in st_p])
    mlstm_n_p = jnp.stack([s[1] for s in st_p])
    mlstm_m_p = jnp.stack([s[2] for s in st_p])
    mem_k_p = jnp.stack([kv[0] for kv in mem_kv_p])
    mem_v_p = jnp.stack([kv[1] for kv in mem_kv_p])
    mlstm_c_s = jnp.stack([s[0] for s in st_s])
    mlstm_n_s = jnp.stack([s[1] for s in st_s])
    mlstm_m_s = jnp.stack([s[2] for s in st_s])
    return (y_prompt, y_sample, mlstm_c_p, mlstm_n_p, mlstm_m_p, k_p, v_p, mem_k_p, mem_v_p,
            mlstm_c_s, mlstm_n_s, mlstm_m_s, k_s, v_s)
```

```python
import functools
import math

import jax
import jax.numpy as jnp
from jax import lax
from jax.experimental import pallas as pl
from jax.experimental.pallas import tpu as pltpu

D_MODEL = 4096
DEPTH = 2
PAGE_SIZE = 128
MIX_TOKEN = 3 * D_MODEL // 4
MIX_MEM = D_MODEL // 4
M_HEADS = 6
M_DV = MIX_TOKEN // M_HEADS
M_DQK = M_DV // 2
M_CHUNK_MAX = 256
M_CHUNK_MIN = 128
B_HEADS = 24
B_DH = MIX_TOKEN // B_HEADS
MOBA_BLOCK = 256
MOBA_TOPK = 3
MEM_SLOTS = 256
MEM_HEADS = 4
MEM_DH = MIX_MEM // MEM_HEADS
REL_BUCKETS = 32
REL_MAX_DIST = 4096
PEER_HEADS = 8
PEER_NKEYS = 128
PEER_N = PEER_NKEYS * PEER_NKEYS
PEER_DKEY = 128
PEER_TOPK = 16
DN_ALPHA = (2.0 * DEPTH) ** 0.25
LN_EPS = 1e-5

VMEM_LIMIT_V7X = 56 * 1024 * 1024
NEG_BIG = -1e30

_NT = (((1,), (1,)), ((), ()))
_TN = (((0,), (0,)), ((), ()))
_BF = jnp.bfloat16
_F32 = jnp.float32


def _params(*sem):
    return pltpu.CompilerParams(dimension_semantics=sem, vmem_limit_bytes=VMEM_LIMIT_V7X)


def _row_to_col(row, n):
    eye = lax.broadcasted_iota(jnp.int32, (n, n), 0) == lax.broadcasted_iota(jnp.int32, (n, n), 1)
    return jnp.sum(jnp.where(eye, jnp.broadcast_to(row, (n, n)), 0.0), axis=1, keepdims=True)


def _col_to_row(col, n):
    eye = lax.broadcasted_iota(jnp.int32, (n, n), 0) == lax.broadcasted_iota(jnp.int32, (n, n), 1)
    return jnp.sum(jnp.where(eye, jnp.broadcast_to(col, (n, n)), 0.0), axis=0, keepdims=True)


def _mm_kernel(*refs, n_pairs, precision):
    o_ref = refs[2 * n_pairs]
    acc = None
    for a_ref, w_ref in zip(refs[:n_pairs], refs[n_pairs:2 * n_pairs]):
        a = a_ref[...]
        if precision is None:
            a = a.astype(_BF)
        d = jnp.dot(a, w_ref[...], preferred_element_type=_F32, precision=precision)
        acc = d if acc is None else acc + d
    o_ref[...] = acc


def _matmul(a_list, w_list, tm, tn, precision=None):
    n = a_list[0].shape[0]
    n_out = w_list[0].shape[1]
    tm = min(tm, n)
    tn = min(tn, n_out)
    in_specs = [pl.BlockSpec((tm, a.shape[1]), lambda i, j: (i, 0)) for a in a_list]
    in_specs += [pl.BlockSpec((w.shape[0], tn), lambda i, j: (0, j)) for w in w_list]
    return pl.pallas_call(
        functools.partial(_mm_kernel, n_pairs=len(a_list), precision=precision),
        grid=(n // tm, n_out // tn),
        in_specs=in_specs,
        out_specs=pl.BlockSpec((tm, tn), lambda i, j: (i, j)),
        out_shape=jax.ShapeDtypeStruct((n, n_out), _F32),
        compiler_params=_params("parallel", "parallel"),
    )(*a_list, *w_list)


def _ln_kernel(x_ref, y_ref, g_ref, b_ref, o_ref, ob_ref):
    z = DN_ALPHA * x_ref[...] + y_ref[...]
    mu = jnp.mean(z, axis=-1, keepdims=True)
    zc = z - mu
    var = jnp.mean(zc * zc, axis=-1, keepdims=True)
    y = zc * lax.rsqrt(var + LN_EPS) * g_ref[...] + b_ref[...]
    o_ref[...] = y
    ob_ref[...] = y.astype(_BF)


def _add_layernorm(x, y, g, b):
    n, d = x.shape
    tm = min(128, n)
    row = pl.BlockSpec((tm, d), lambda i: (i, 0))
    vec = pl.BlockSpec((1, d), lambda i: (0, 0))
    return pl.pallas_call(
        _ln_kernel,
        grid=(n // tm,),
        in_specs=[row, row, vec, vec],
        out_specs=[row, row],
        out_shape=[jax.ShapeDtypeStruct((n, d), _F32), jax.ShapeDtypeStruct((n, d), _BF)],
        compiler_params=_params("parallel"),
    )(x, y, g.reshape(1, d), b.reshape(1, d))


def _log_sigmoid(x):
    return jnp.minimum(x, 0.0) - jnp.log1p(jnp.exp(-jnp.abs(x)))


def _mlstm_kernel(bif_ref, q_ref, k_ref, v_ref, og_ref, gi_ref, gf_ref, c0_ref, n0_ref, m0_ref,
                  tok_ref, c_ref, n_ref, m_ref, *, l_in, l):
    h = pl.program_id(1)

    @pl.when(pl.program_id(2) == 0)
    def _():
        c_ref[...] = c0_ref[...]
        n_ref[...] = n0_ref[...]
        m_ref[...] = m0_ref[...]

    def rows(ref):
        x = ref[...]
        if l_in == l:
            return x
        return jnp.concatenate([x, jnp.zeros((l - l_in, x.shape[1]), x.dtype)], axis=0)

    qf = rows(q_ref)
    kf = rows(k_ref) * (M_DQK ** -0.5)
    vb = rows(v_ref).astype(_BF)
    i_row = gi_ref[0, 0] + bif_ref[h]
    f_row = gf_ref[0, 0] + bif_ref[M_HEADS + h]
    lf_row = _log_sigmoid(f_row)

    t_idx = lax.broadcasted_iota(jnp.int32, (l, l), 0)
    s_idx = lax.broadcasted_iota(jnp.int32, (l, l), 1)
    causal = s_idx <= t_idx
    b_col = jnp.sum(jnp.where(causal, jnp.broadcast_to(lf_row, (l, l)), 0.0), axis=1, keepdims=True)
    b_row = _col_to_row(b_col, l)
    i_col = _row_to_col(i_row, l)

    m_prev = m_ref[0, 0, :, 0:1]
    dmat = jnp.where(causal, b_col - b_row + i_row, -jnp.inf)
    inter = b_col + m_prev
    m_t = jnp.maximum(inter, jnp.max(dmat, axis=1, keepdims=True))
    w_inter = jnp.exp(inter - m_t)

    qb = qf.astype(_BF)
    kb = kf.astype(_BF)
    s = lax.dot_general(qb, kb, _NT, preferred_element_type=_F32) * jnp.exp(dmat - m_t)
    c_old = c_ref[0, 0]
    n_old = n_ref[0, 0]
    num = (jnp.dot(s.astype(_BF), vb, preferred_element_type=_F32)
           + w_inter * jnp.dot(qb, c_old.astype(_BF), preferred_element_type=_F32))
    den = jnp.sum(s, axis=1, keepdims=True) + w_inter * jnp.sum(qf * n_old, axis=1, keepdims=True)
    hid = num / jnp.maximum(jnp.abs(den), jnp.exp(-m_t))
    tok = jax.nn.sigmoid(rows(og_ref)) * hid
    tok_ref[...] = tok[0:l_in]

    m_new = m_t[l - 1:l, :]
    b_last = b_col[l - 1:l, :]
    w_end = jnp.exp(b_last - b_col + i_col - m_new)
    decay = jnp.exp(b_last + m_prev - m_new)
    kw = kf * w_end
    c_ref[0, 0] = decay * c_old + lax.dot_general(kw.astype(_BF), vb, _TN, preferred_element_type=_F32)
    n_ref[0, 0] = decay * n_old + jnp.sum(kw, axis=0, keepdims=True)
    m_ref[0, 0] = jnp.broadcast_to(m_new, (1, 128))


def _mlstm(u, gates_t, b_if, c0, n0, m0, batch, seq):
    n = batch * seq
    l = min(seq, M_CHUNK_MAX)
    nc = seq // l
    l_pad = max(l, M_CHUNK_MIN)
    gi = gates_t[:M_HEADS].reshape(M_HEADS, n // l, 1, l)
    gf = gates_t[M_HEADS:].reshape(M_HEADS, n // l, 1, l)
    if l_pad != l:
        gi = jnp.pad(gi, ((0, 0), (0, 0), (0, 0), (0, l_pad - l)), constant_values=NEG_BIG)
        gf = jnp.pad(gf, ((0, 0), (0, 0), (0, 0), (0, l_pad - l)), constant_values=-NEG_BIG)
    vq = M_DV // M_DQK
    row = lambda b, h, c: b * nc + c
    state_c = pl.BlockSpec((1, 1, M_DQK, M_DV), lambda b, h, c: (b, h, 0, 0))
    state_n = pl.BlockSpec((1, 1, 1, M_DQK), lambda b, h, c: (b, h, 0, 0))
    state_m = pl.BlockSpec((1, 1, 1, 128), lambda b, h, c: (b, h, 0, 0))
    gate = pl.BlockSpec((1, 1, 1, l_pad), lambda b, h, c: (h, row(b, h, c), 0, 0))
    tok, c_new, n_new, m_new = pl.pallas_call(
        functools.partial(_mlstm_kernel, l_in=l, l=l_pad),
        grid=(batch, M_HEADS, nc),
        in_specs=[
            pl.BlockSpec(memory_space=pltpu.SMEM),
            pl.BlockSpec((l, M_DQK), lambda b, h, c: (row(b, h, c), h)),
            pl.BlockSpec((l, M_DQK), lambda b, h, c: (row(b, h, c), M_HEADS + h)),
            pl.BlockSpec((l, M_DV), lambda b, h, c: (row(b, h, c), M_HEADS + h)),
            pl.BlockSpec((l, M_DV), lambda b, h, c: (row(b, h, c), 2 * M_HEADS + h)),
            gate, gate, state_c, state_n, state_m,
        ],
        out_specs=[pl.BlockSpec((l, M_DV), lambda b, h, c: (row(b, h, c), h)), state_c, state_n, state_m],
        out_shape=[
            jax.ShapeDtypeStruct((n, MIX_TOKEN), _F32),
            jax.ShapeDtypeStruct((batch, M_HEADS, M_DQK, M_DV), _F32),
            jax.ShapeDtypeStruct((batch, M_HEADS, 1, M_DQK), _F32),
            jax.ShapeDtypeStruct((batch, M_HEADS, 1, 128), _F32),
        ],
        compiler_params=_params("parallel", "parallel", "arbitrary"),
    )(b_if, u, u, u, u, gi, gf, c0, n0.reshape(batch, M_HEADS, 1, M_DQK),
      jnp.broadcast_to(m0[:, :, None, None], (batch, M_HEADS, 1, 128)))
    assert vq * M_DQK == M_DV
    return tok, (c_new, n_new.reshape(batch, M_HEADS, M_DQK), m_new[:, :, 0, 0])


def _memattn_kernel(q_ref, k_ref, v_ref, o_ref):
    q = q_ref[...].astype(_BF)
    logits = lax.dot_general(q, k_ref[0].astype(_BF), _NT, preferred_element_type=_F32) * (MEM_DH ** -0.5)
    e = jnp.exp(logits - jnp.max(logits, axis=-1, keepdims=True))
    p = e / jnp.sum(e, axis=-1, keepdims=True)
    o_ref[...] = jnp.dot(p.astype(_BF), v_ref[0].astype(_BF), preferred_element_type=_F32)


def _mem_attend(qsrc, q_col0, mk, mv, k_col0, v_col0, batch, seq):
    n = batch * seq
    tq = min(seq, 512)
    nq = seq // tq
    qc, kc, vc = q_col0 // MEM_DH, k_col0 // MEM_DH, v_col0 // MEM_DH
    return pl.pallas_call(
        _memattn_kernel,
        grid=(batch, nq, MEM_HEADS),
        in_specs=[
            pl.BlockSpec((tq, MEM_DH), lambda b, t, h: (b * nq + t, qc + h)),
            pl.BlockSpec((1, MEM_SLOTS, MEM_DH), lambda b, t, h: (b, 0, kc + h)),
            pl.BlockSpec((1, MEM_SLOTS, MEM_DH), lambda b, t, h: (b, 0, vc + h)),
        ],
        out_specs=pl.BlockSpec((tq, MEM_DH), lambda b, t, h: (b * nq + t, h)),
        out_shape=jax.ShapeDtypeStruct((n, MIX_MEM), _F32),
        compiler_params=_params("parallel", "parallel", "parallel"),
    )(qsrc, mk, mv)


def _rel_bucket(dist):
    n = jnp.maximum(dist, 0)
    max_exact = REL_BUCKETS // 2
    nf = jnp.maximum(n, 1).astype(_F32)
    large = max_exact + (jnp.log(nf / max_exact) / math.log(REL_MAX_DIST / max_exact)
                         * (REL_BUCKETS - max_exact)).astype(jnp.int32)
    return jnp.where(n < max_exact, n, jnp.minimum(large, REL_BUCKETS - 1))


def _bias_lookup(rb_ref, head, dist):
    bucket = _rel_bucket(dist)
    val = jnp.full(dist.shape, rb_ref[0, head], _F32)
    for b in range(1, REL_BUCKETS):
        val = jnp.where(bucket == b, rb_ref[b, head], val)
    return val


def _bias_tiles_kernel(rb_ref, o_ref):
    head, delta = pl.program_id(0), pl.program_id(1)
    key = lax.broadcasted_iota(jnp.int32, (MOBA_BLOCK, MOBA_BLOCK), 0)
    qry = lax.broadcasted_iota(jnp.int32, (MOBA_BLOCK, MOBA_BLOCK), 1)
    o_ref[0, 0] = _bias_lookup(rb_ref, head, delta * MOBA_BLOCK + qry - key)


def _bias_tiles(rel_bias, nb):
    return pl.pallas_call(
        _bias_tiles_kernel,
        grid=(B_HEADS, nb),
        in_specs=[pl.BlockSpec(memory_space=pltpu.SMEM)],
        out_specs=pl.BlockSpec((1, 1, MOBA_BLOCK, MOBA_BLOCK), lambda h, d: (h, d, 0, 0)),
        out_shape=jax.ShapeDtypeStruct((B_HEADS, nb, MOBA_BLOCK, MOBA_BLOCK), _F32),
        compiler_params=_params("parallel", "parallel"),
    )(rel_bias)


def _moba_prompt_kernel(q_ref, k_ref, v_ref, bias_ref, o_ref, s_scr, sel_scr, *, nb):
    j = pl.program_id(2)
    blk = MOBA_BLOCK
    qf = q_ref[...]
    qb = qf.astype(_BF)

    means = jnp.concatenate(
        [jnp.mean(k_ref[c * blk:(c + 1) * blk, :], axis=0, keepdims=True) for c in range(nb)], axis=0)
    gate = lax.dot_general(means, qf, _NT, preferred_element_type=_F32, precision=lax.Precision.HIGHEST)
    g = [gate[c:c + 1, :] for c in range(nb)]
    past = [jnp.where(c < j, 1.0, 0.0) for c in range(nb)]
    for c in range(nb):
        rank = jnp.zeros((1, blk), _F32)
        for c2 in range(nb):
            if c2 == c:
                continue
            beats = g[c2] >= g[c] if c2 < c else g[c2] > g[c]
            rank = rank + jnp.where(beats, 1.0, 0.0) * past[c2]
        chosen = jnp.where(rank < MOBA_TOPK, 1.0, 0.0) * past[c]
        sel_scr[c:c + 1, :] = jnp.where(chosen > 0.0, 0.0, -jnp.inf)

    key = lax.broadcasted_iota(jnp.int32, (blk, blk), 0)
    qry = lax.broadcasted_iota(jnp.int32, (blk, blk), 1)
    causal_pen = jnp.where(key <= qry, 0.0, -jnp.inf)
    scale = B_DH ** -0.5

    def logits(c, m_run):
        kc = k_ref[pl.ds(pl.multiple_of(c * blk, blk), blk), :].astype(_BF)
        st = lax.dot_general(kc, qb, _NT, preferred_element_type=_F32) * scale + bias_ref[0, j - c]
        st = st + jnp.where(c == j, causal_pen, jnp.broadcast_to(sel_scr[pl.ds(c, 1), :], (blk, blk)))
        s_scr[c] = st
        return jnp.maximum(m_run, jnp.max(st, axis=0, keepdims=True))

    m_run = lax.fori_loop(0, j + 1, logits, jnp.full((1, blk), -jnp.inf, _F32))

    def weigh(c, carry):
        l_run, acc = carry
        p = jnp.exp(s_scr[c] - m_run)
        vc = v_ref[pl.ds(pl.multiple_of(c * blk, blk), blk), :].astype(_BF)
        acc = acc + lax.dot_general(p.astype(_BF), vc, _TN, preferred_element_type=_F32)
        return l_run + jnp.sum(p, axis=0, keepdims=True), acc

    l_run, acc = lax.fori_loop(0, j + 1, weigh, (jnp.zeros((1, blk), _F32), jnp.zeros((blk, B_DH), _F32)))
    o_ref[...] = acc / _row_to_col(l_run, blk)


def _moba_prompt(qsrc, k, v, bias, batch, seq):
    n = batch * seq
    nb = seq // MOBA_BLOCK
    return pl.pallas_call(
        functools.partial(_moba_prompt_kernel, nb=nb),
        grid=(B_HEADS, batch, nb),
        in_specs=[
            pl.BlockSpec((MOBA_BLOCK, B_DH), lambda h, b, j: (b * nb + j, h)),
            pl.BlockSpec((seq, B_DH), lambda h, b, j: (b, h)),
            pl.BlockSpec((seq, B_DH), lambda h, b, j: (b, h)),
            pl.BlockSpec((1, nb, MOBA_BLOCK, MOBA_BLOCK), lambda h, b, j: (h, 0, 0, 0)),
        ],
        out_specs=pl.BlockSpec((MOBA_BLOCK, B_DH), lambda h, b, j: (b * nb + j, h)),
        out_shape=jax.ShapeDtypeStruct((n, MIX_TOKEN), _F32),
        scratch_shapes=[pltpu.VMEM((nb, MOBA_BLOCK, MOBA_BLOCK), _F32), pltpu.VMEM((nb, MOBA_BLOCK), _F32)],
        compiler_params=_params("parallel", "parallel", "parallel"),
    )(qsrc, k, v, bias)


def _page_means_kernel(pt_ref, ka_ref, kb_ref, o_ref):
    total = jnp.sum(ka_ref[0], axis=0, keepdims=True) + jnp.sum(kb_ref[0], axis=0, keepdims=True)
    o_ref[0, 0] = total / MOBA_BLOCK


def _page_means(k_pool, page_table, nfull):
    batch = page_table.shape[0]
    width = k_pool.shape[2]
    grid_spec = pltpu.PrefetchScalarGridSpec(
        num_scalar_prefetch=1,
        grid=(batch, nfull),
        in_specs=[
            pl.BlockSpec((1, PAGE_SIZE, width), lambda b, c, pt: (pt[b, 2 * c], 0, 0)),
            pl.BlockSpec((1, PAGE_SIZE, width), lambda b, c, pt: (pt[b, 2 * c + 1], 0, 0)),
        ],
        out_specs=pl.BlockSpec((1, 1, 1, width), lambda b, c, pt: (b, c, 0, 0)),
    )
    out = pl.pallas_call(
        _page_means_kernel,
        grid_spec=grid_spec,
        out_shape=jax.ShapeDtypeStruct((batch, nfull, 1, width), _F32),
        compiler_params=_params("parallel", "parallel"),
    )(page_table, k_pool, k_pool)
    return out.reshape(batch, nfull, width)


def _moba_sample_select_kernel(q_ref, means_ref, sel_ref, *, seq, nfull):
    lanes = 128
    col = lax.broadcasted_iota(jnp.int32, (seq, lanes), 1)
    colf = col.astype(_F32)
    out = jnp.zeros((seq, lanes), _F32)
    for h in range(B_HEADS):
        qh = q_ref[:, h * B_DH:(h + 1) * B_DH]
        mh = means_ref[0, :, h * B_DH:(h + 1) * B_DH]
        mh = jnp.concatenate([mh, jnp.zeros((lanes - nfull, B_DH), _F32)], axis=0)
        gate = lax.dot_general(qh, mh, _NT, preferred_element_type=_F32, precision=lax.Precision.HIGHEST)
        gate = jnp.where(col < nfull, gate, -jnp.inf)
        for kk in range(MOBA_TOPK):
            best = jnp.max(gate, axis=1, keepdims=True)
            idx = jnp.min(jnp.where(gate == best, colf, float(lanes)), axis=1, keepdims=True)
            out = jnp.where(col == h * MOBA_TOPK + kk, idx, out)
            gate = jnp.where(colf == idx, -jnp.inf, gate)
    sel_ref[0] = out.astype(jnp.int32)


def _moba_sample_select(qsrc, means, batch, seq):
    nfull = means.shape[1]
    assert nfull >= MOBA_TOPK and nfull <= 128 and B_HEADS * MOBA_TOPK <= 128
    return pl.pallas_call(
        functools.partial(_moba_sample_select_kernel, seq=seq, nfull=nfull),
        grid=(batch,),
        in_specs=[
            pl.BlockSpec((seq, MIX_TOKEN), lambda b: (b, 0)),
            pl.BlockSpec((1, nfull, MIX_TOKEN), lambda b: (b, 0, 0)),
        ],
        out_specs=pl.BlockSpec((1, seq, 128), lambda b: (b, 0, 0)),
        out_shape=jax.ShapeDtypeStruct((batch, seq, 128), jnp.int32),
        compiler_params=_params("parallel"),
    )(qsrc, means)


def _moba_sample_kernel(sel_ref, pt_ref, q_ref, kn_ref, vn_ref, rb_ref, kpool_ref, vpool_ref, o_ref,
                        kbuf, vbuf, sems, *, seq, past):
    b, h = pl.program_id(0), pl.program_id(1)
    blk = MOBA_BLOCK
    ppb = blk // PAGE_SIZE
    nsel = seq * MOBA_TOPK
    lane0 = pl.multiple_of(h * B_DH, B_DH)

    def copies(t, kk, pg):
        block = sel_ref[b, t, h * MOBA_TOPK + kk]
        page = pt_ref[b, block * ppb + pg]
        dst = pl.ds(((t * MOBA_TOPK + kk) * ppb + pg) * PAGE_SIZE, PAGE_SIZE)
        return (pltpu.make_async_copy(kpool_ref.at[page, :, pl.ds(lane0, B_DH)], kbuf.at[dst, :], sems.at[0]),
                pltpu.make_async_copy(vpool_ref.at[page, :, pl.ds(lane0, B_DH)], vbuf.at[dst, :], sems.at[1]))

    every = [(t, kk, pg) for t in range(seq) for kk in range(MOBA_TOPK) for pg in range(ppb)]
    for idx in every:
        for cp in copies(*idx):
            cp.start()

    qf = q_ref[...]
    qb = qf.astype(_BF)
    scale = B_DH ** -0.5
    pad = jnp.zeros((128 - seq, B_DH), _F32)
    k_own = jnp.concatenate([kn_ref[...], pad], axis=0).astype(_BF)
    v_own = jnp.concatenate([vn_ref[...], pad], axis=0).astype(_BF)
    t_own = lax.broadcasted_iota(jnp.int32, (seq, 128), 0)
    s_own = lax.broadcasted_iota(jnp.int32, (seq, 128), 1)
    lo = (lax.dot_general(qb, k_own, _NT, preferred_element_type=_F32) * scale
          + _bias_lookup(rb_ref, h, t_own - s_own))
    lo = jnp.where(s_own <= t_own, lo, -jnp.inf)

    ncol = nsel * blk
    t_sel = lax.broadcasted_iota(jnp.int32, (seq, ncol), 0)
    col = lax.broadcasted_iota(jnp.int32, (seq, ncol), 1)
    slot = col >> int(math.log2(blk))
    block_of_col = jnp.zeros((seq, ncol), jnp.int32)
    token_of_col = jnp.zeros((seq, ncol), jnp.int32)
    for t in range(seq):
        for kk in range(MOBA_TOPK):
            here = slot == t * MOBA_TOPK + kk
            block_of_col = jnp.where(here, sel_ref[b, t, h * MOBA_TOPK + kk], block_of_col)
            token_of_col = jnp.where(here, t, token_of_col)
    dist = past + t_sel - (block_of_col * blk + (col & (blk - 1)))
    bias_sel = _bias_lookup(rb_ref, h, dist)

    for idx in every:
        for cp in copies(*idx):
            cp.wait()

    ls = lax.dot_general(qb, kbuf[...].astype(_BF), _NT, preferred_element_type=_F32) * scale + bias_sel
    ls = jnp.where(token_of_col == t_sel, ls, -jnp.inf)
    m = jnp.maximum(jnp.max(lo, axis=1, keepdims=True), jnp.max(ls, axis=1, keepdims=True))
    p_own = jnp.exp(lo - m)
    p_sel = jnp.exp(ls - m)
    denom = jnp.sum(p_own, axis=1, keepdims=True) + jnp.sum(p_sel, axis=1, keepdims=True)
    out = (jnp.dot(p_own.astype(_BF), v_own, preferred_element_type=_F32)
           + jnp.dot(p_sel.astype(_BF), vbuf[...].astype(_BF), preferred_element_type=_F32))
    o_ref[...] = out / denom


def _moba_sample(qsrc, k_new, v_new, sel, page_table, rel_bias, k_pool, v_pool, batch, seq):
    n = batch * seq
    past = page_table.shape[1] * PAGE_SIZE
    assert past % MOBA_BLOCK == 0 and (past + seq - 1) // MOBA_BLOCK == past // MOBA_BLOCK
    rows = seq * MOBA_TOPK * MOBA_BLOCK
    grid_spec = pltpu.PrefetchScalarGridSpec(
        num_scalar_prefetch=2,
        grid=(batch, B_HEADS),
        in_specs=[
            pl.BlockSpec((seq, B_DH), lambda b, h, *_: (b, h)),
            pl.BlockSpec((seq, B_DH), lambda b, h, *_: (b, h)),
            pl.BlockSpec((seq, B_DH), lambda b, h, *_: (b, h)),
            pl.BlockSpec(memory_space=pltpu.SMEM),
            pl.BlockSpec(memory_space=pl.ANY),
            pl.BlockSpec(memory_space=pl.ANY),
        ],
        out_specs=pl.BlockSpec((seq, B_DH), lambda b, h, *_: (b, h)),
        scratch_shapes=[pltpu.VMEM((rows, B_DH), _F32), pltpu.VMEM((rows, B_DH), _F32),
                        pltpu.SemaphoreType.DMA((2,))],
    )
    return pl.pallas_call(
        functools.partial(_moba_sample_kernel, seq=seq, past=past),
        grid_spec=grid_spec,
        out_shape=jax.ShapeDtypeStruct((n, MIX_TOKEN), _F32),
        compiler_params=_params("arbitrary", "arbitrary"),
    )(sel, page_table, qsrc, k_new, v_new, rel_bias, k_pool, v_pool)


def _top_sorted(scores, count):
    tm = scores.shape[1]
    rank = lax.broadcasted_iota(jnp.int32, (count, tm), 0)
    vals = []
    stacked = jnp.zeros((count, tm), _F32)
    cur = scores
    for r in range(count):
        best = jnp.max(cur, axis=0, keepdims=True)
        vals.append(best)
        stacked = jnp.where(rank == r, best, stacked)
        cur = jnp.where(cur >= best, -jnp.inf, cur)
    return vals, stacked


def _peer_candidates(a, a_st, b, b_st):
    tm = a_st.shape[1]
    half = PEER_TOPK // 2
    groups = [a[0] + b_st[0:half], a[0] + b_st[half:PEER_TOPK]]
    groups += [a[i] + b_st[0:half] for i in range(1, half)]
    groups.append(a_st[half:PEER_TOPK] + b[0])
    return jnp.concatenate([jnp.broadcast_to(g, (half, tm)) for g in groups], axis=0)


def _peer_select_kernel(q_ref, keys_ref, s1_ref, s2_ref, e1_ref, e2_ref, tau_ref):
    tm = q_ref.shape[0]
    for h in range(PEER_HEADS):
        qh = q_ref[:, h * PEER_DKEY:(h + 1) * PEER_DKEY]
        st = lax.dot_general(keys_ref[h], qh, _NT, preferred_element_type=_F32,
                             precision=lax.Precision.HIGHEST)
        s1 = st[0:PEER_NKEYS]
        s2 = st[PEER_NKEYS:2 * PEER_NKEYS]
        a, a_st = _top_sorted(s1, PEER_TOPK)
        bb, b_st = _top_sorted(s2, PEER_TOPK)
        cand = _peer_candidates(a, a_st, bb, b_st)
        tau = jnp.zeros((1, tm), _F32)
        taken = jnp.zeros((1, tm), _F32)
        cur = cand
        for _ in range(PEER_TOPK):
            best = jnp.max(cur, axis=0, keepdims=True)
            hit = cur == best
            tau = jnp.where(taken < PEER_TOPK, best, tau)
            taken = taken + jnp.sum(jnp.where(hit, 1.0, 0.0), axis=0, keepdims=True)
            cur = jnp.where(hit, -jnp.inf, cur)
        top = a[0] + bb[0]
        z = jnp.sum(jnp.where(cand >= tau, jnp.exp(cand - top), 0.0), axis=0, keepdims=True)
        s1_ref[h] = s1
        s2_ref[h] = s2
        e1_ref[h] = jnp.exp(s1 - a[0]) / z
        e2_ref[h] = jnp.exp(s2 - bb[0])
        tau_ref[h] = jnp.broadcast_to(tau, (8, tm))


def _peer_select(q, keys_bd):
    n = q.shape[0]
    tm = min(256, n)
    half = pl.BlockSpec((PEER_HEADS, PEER_NKEYS, tm), lambda i: (0, 0, i))
    half_shape = jax.ShapeDtypeStruct((PEER_HEADS, PEER_NKEYS, n), _F32)
    return pl.pallas_call(
        _peer_select_kernel,
        grid=(n // tm,),
        in_specs=[pl.BlockSpec((tm, PEER_HEADS * PEER_DKEY), lambda i: (i, 0)),
                  pl.BlockSpec((PEER_HEADS, 2 * PEER_NKEYS, PEER_DKEY), lambda i: (0, 0, 0))],
        out_specs=[half, half, half, half, pl.BlockSpec((PEER_HEADS, 8, tm), lambda i: (0, 0, i))],
        out_shape=[half_shape, half_shape, half_shape, half_shape,
                   jax.ShapeDtypeStruct((PEER_HEADS, 8, n), _F32)],
        compiler_params=_params("parallel"),
    )(q, keys_bd)


def _peer_dense_kernel(x_ref, u_ref, v_ref, s1_ref, s2_ref, e1_ref, e2_ref, tau_ref, o_ref, h_scr, *, te):
    e = pl.program_id(1)
    tm = x_ref.shape[0]
    a_t = lax.dot_general(u_ref[...], x_ref[...], _NT, preferred_element_type=_F32)
    for r in range(te // PEER_NKEYS):
        i1 = e * (te // PEER_NKEYS) + r
        w = jnp.zeros((PEER_NKEYS, tm), _F32)
        for h in range(PEER_HEADS):
            total = s1_ref[h, pl.ds(i1, 1), :] + s2_ref[h]
            w = w + jnp.where(total >= tau_ref[h, 0:1, :], e2_ref[h] * e1_ref[h, pl.ds(i1, 1), :], 0.0)
        act = jax.nn.gelu(a_t[r * PEER_NKEYS:(r + 1) * PEER_NKEYS])
        h_scr[r * PEER_NKEYS:(r + 1) * PEER_NKEYS, :] = (act * w).astype(_BF)
    contrib = lax.dot_general(h_scr[...], v_ref[...], _TN, preferred_element_type=_F32)

    @pl.when(e == 0)
    def _():
        o_ref[...] = contrib

    @pl.when(e != 0)
    def _():
        o_ref[...] += contrib


def _peer_dense(xb, u_b, v_b, sel):
    n, d = xb.shape
    tm = min(256, n)
    te = 512
    half = pl.BlockSpec((PEER_HEADS, PEER_NKEYS, tm), lambda i, e: (0, 0, i))
    return pl.pallas_call(
        functools.partial(_peer_dense_kernel, te=te),
        grid=(n // tm, PEER_N // te),
        in_specs=[pl.BlockSpec((tm, d), lambda i, e: (i, 0)),
                  pl.BlockSpec((te, d), lambda i, e: (e, 0)),
                  pl.BlockSpec((te, d), lambda i, e: (e, 0)),
                  half, half, half, half,
                  pl.BlockSpec((PEER_HEADS, 8, tm), lambda i, e: (0, 0, i))],
        out_specs=pl.BlockSpec((tm, d), lambda i, e: (i, 0)),
        out_shape=jax.ShapeDtypeStruct((n, d), _F32),
        scratch_shapes=[pltpu.VMEM((te, tm), _BF)],
        compiler_params=_params("parallel", "arbitrary"),
    )(xb, u_b, v_b, *sel)


def _peer(xb, wq_b, keys_bd, u_b, v_b):
    n = xb.shape[0]
    n_pad = max(n, 128)
    if n_pad != n:
        xb = jnp.pad(xb, ((0, n_pad - n), (0, 0)))
    q = _matmul([xb], [wq_b], 1024, 512)
    out = _peer_dense(xb, u_b, v_b, _peer_select(q, keys_bd))
    return out[:n]


def _layer_tail(x, tok, mem, w_out_b, ln1, peer_w, ln2):
    mixed = _matmul([tok, mem], [w_out_b[:MIX_TOKEN], w_out_b[MIX_TOKEN:]], 512, 512)
    x1, x1b = _add_layernorm(x, mixed, *ln1)
    x2, x2b = _add_layernorm(x1, _peer(x1b, *peer_w), *ln2)
    return x2, x2b


def _trunk(x, batch, seq, mem_kv, mlstm_init, moba_attend, w):
    xb = x.astype(_BF)
    u = _matmul([xb], [w['in_a_main']], 1024, 512)
    gates = _matmul([x], [w['in_a_gates']], 512, 128, precision=lax.Precision.HIGHEST)
    tok, state = _mlstm(u, gates[:, :2 * M_HEADS].T, w['b_if'], *mlstm_init, batch, seq)
    mem = _mem_attend(u, 2 * M_HEADS * M_DQK + 2 * MIX_TOKEN, *mem_kv[0], batch, seq)
    x, xb = _layer_tail(x, tok, mem, w['out'][0], w['ln1'][0], w['peer'][0], w['ln2'][0])
    k_sh = _matmul([xb], [w['k_shared']], 1024, 512)
    v_sh = _matmul([xb], [w['v_shared']], 1024, 512)
    ub = _matmul([xb], [w['in_b']], 1024, 512)
    tok = moba_attend(ub, k_sh, v_sh)
    mem = _mem_attend(ub, MIX_TOKEN, *mem_kv[1], batch, seq)
    x, _ = _layer_tail(x, tok, mem, w['out'][1], w['ln1'][1], w['peer'][1], w['ln2'][1])
    return x, state, k_sh, v_sh


def kernel(x_prompt, x_sample, cache_moba_k, cache_moba_v, cache_mem_k, cache_mem_v, state_mlstm_c,
           state_mlstm_n, state_mlstm_m, page_table, mem_prompt, w_in_a, b_if_a, w_in_b, w_kv_shared, rel_bias,
           w_mem_kv, w_out, ln1_g, ln1_b, ln2_g, ln2_b, peer_wq, peer_keys, peer_u, peer_v):
    bp, tp, d = x_prompt.shape
    bs, ts, _ = x_sample.shape
    gate0 = 2 * M_HEADS * M_DQK + 2 * MIX_TOKEN
    wa = w_in_a[0]
    zeros = jnp.zeros((PEER_HEADS, PEER_NKEYS, PEER_DKEY // 2), _F32)

    def keys_blockdiag(keys):
        return jnp.concatenate([jnp.concatenate([keys[:, 0], zeros], axis=2),
                                jnp.concatenate([zeros, keys[:, 1]], axis=2)], axis=1)

    w = {
        'in_a_main': jnp.concatenate([wa[:, :gate0], wa[:, gate0 + 2 * M_HEADS:]], axis=1).astype(_BF),
        'in_a_gates': jnp.pad(wa[:, gate0:gate0 + 2 * M_HEADS], ((0, 0), (0, 128 - 2 * M_HEADS))),
        'b_if': b_if_a[0],
        'in_b': w_in_b[0].astype(_BF),
        'k_shared': w_kv_shared[:, :MIX_TOKEN].astype(_BF),
        'v_shared': w_kv_shared[:, MIX_TOKEN:].astype(_BF),
        'out': [w_out[l].astype(_BF) for l in range(DEPTH)],
        'ln1': [(ln1_g[l], ln1_b[l]) for l in range(DEPTH)],
        'ln2': [(ln2_g[l], ln2_b[l]) for l in range(DEPTH)],
        'peer': [(peer_wq[l].astype(_BF), keys_blockdiag(peer_keys[l]), peer_u[l].astype(_BF),
                  peer_v[l].astype(_BF)) for l in range(DEPTH)],
    }

    mem_w = jnp.concatenate([w_mem_kv[l] for l in range(DEPTH)], axis=1).astype(_BF)
    mkv = _matmul([mem_prompt.reshape(bp * MEM_SLOTS, d).astype(_BF)], [mem_w], 1024, 512)
    mkv3 = mkv.reshape(bp, MEM_SLOTS, 2 * DEPTH * MIX_MEM)
    mem_kv_p = [(mkv3, mkv3, 2 * l * MIX_MEM, (2 * l + 1) * MIX_MEM) for l in range(DEPTH)]
    init_p = (jnp.zeros((bp, M_HEADS, M_DQK, M_DV), _F32), jnp.zeros((bp, M_HEADS, M_DQK), _F32),
              jnp.zeros((bp, M_HEADS), _F32))
    bias = _bias_tiles(rel_bias, tp // MOBA_BLOCK)
    y_p, st_p, k_p, v_p = _trunk(
        x_prompt.reshape(bp * tp, d), bp, tp, mem_kv_p, init_p,
        lambda ub, k_sh, v_sh: _moba_prompt(ub, k_sh, v_sh, bias, bp, tp), w)

    mem_kv_s = [(cache_mem_k[l].reshape(bs, MEM_SLOTS, MIX_MEM), cache_mem_v[l].reshape(bs, MEM_SLOTS, MIX_MEM), 0, 0)
                for l in range(DEPTH)]
    init_s = (state_mlstm_c[0], state_mlstm_n[0], state_mlstm_m[0])
    n_pool = cache_moba_k.shape[0]
    k_pool = cache_moba_k.reshape(n_pool, PAGE_SIZE, MIX_TOKEN)
    v_pool = cache_moba_v.reshape(n_pool, PAGE_SIZE, MIX_TOKEN)
    nfull = page_table.shape[1] * PAGE_SIZE // MOBA_BLOCK

    def moba_sample(ub, k_sh, v_sh):
        means = _page_means(k_pool, page_table, nfull)
        sel = _moba_sample_select(ub, means, bs, ts)
        return _moba_sample(ub, k_sh, v_sh, sel, page_table, rel_bias, k_pool, v_pool, bs, ts)

    y_s, st_s, k_s, v_s = _trunk(x_sample.reshape(bs * ts, d), bs, ts, mem_kv_s, init_s, moba_sample, w)

    mem_k_p = jnp.stack([mkv3[:, :, 2 * l * MIX_MEM:(2 * l + 1) * MIX_MEM] for l in range(DEPTH)])
    mem_v_p = jnp.stack([mkv3[:, :, (2 * l + 1) * MIX_MEM:(2 * l + 2) * MIX_MEM] for l in range(DEPTH)])
    kv_shape = (bp, MEM_SLOTS, MEM_HEADS, MEM_DH)
    return (y_p.reshape(bp, tp, d), y_s.reshape(bs, ts, d),
            st_p[0][None], st_p[1][None], st_p[2][None],
            k_p.reshape(bp, tp, B_HEADS, B_DH), v_p.reshape(bp, tp, B_HEADS, B_DH),
            mem_k_p.reshape((DEPTH,) + kv_shape), mem_v_p.reshape((DEPTH,) + kv_shape),
            st_s[0][None], st_s[1][None], st_s[2][None],
            k_s.reshape(bs, ts, B_HEADS, B_DH), v_s.reshape(bs, ts, B_HEADS, B_DH))
```

```python
import functools
import math

import jax
import jax.numpy as jnp
from jax import lax
from jax.experimental import pallas as pl
from jax.experimental.pallas import tpu as pltpu

D_MODEL = 4096
DEPTH = 2
PAGE_SIZE = 128
MIX_TOKEN = 3 * D_MODEL // 4
MIX_MEM = D_MODEL // 4
M_HEADS = 6
M_DV = MIX_TOKEN // M_HEADS
M_DQK = M_DV // 2
M_CHUNK_MAX = 256
M_CHUNK_MIN = 128
B_HEADS = 24
B_DH = MIX_TOKEN // B_HEADS
MOBA_BLOCK = 256
MOBA_TOPK = 3
MEM_SLOTS = 256
MEM_HEADS = 4
MEM_DH = MIX_MEM // MEM_HEADS
REL_BUCKETS = 32
REL_MAX_DIST = 4096
PEER_HEADS = 8
PEER_NKEYS = 128
PEER_N = PEER_NKEYS * PEER_NKEYS
PEER_DKEY = 128
PEER_TOPK = 16
DN_ALPHA = (2.0 * DEPTH) ** 0.25
LN_EPS = 1e-5

VMEM_LIMIT_V7X = 56 * 1024 * 1024
NEG_BIG = -1e30

_NT = (((1,), (1,)), ((), ()))
_TN = (((0,), (0,)), ((), ()))
_BF = jnp.bfloat16
_F32 = jnp.float32


def _params(*sem):
    return pltpu.CompilerParams(dimension_semantics=sem, vmem_limit_bytes=VMEM_LIMIT_V7X)


def _row_to_col(row, n):
    eye = lax.broadcasted_iota(jnp.int32, (n, n), 0) == lax.broadcasted_iota(jnp.int32, (n, n), 1)
    return jnp.sum(jnp.where(eye, jnp.broadcast_to(row, (n, n)), 0.0), axis=1, keepdims=True)


def _col_to_row(col, n):
    eye = lax.broadcasted_iota(jnp.int32, (n, n), 0) == lax.broadcasted_iota(jnp.int32, (n, n), 1)
    return jnp.sum(jnp.where(eye, jnp.broadcast_to(col, (n, n)), 0.0), axis=0, keepdims=True)


def _mm_kernel(*refs, n_pairs, precision):
    o_ref = refs[2 * n_pairs]
    acc = None
    for a_ref, w_ref in zip(refs[:n_pairs], refs[n_pairs:2 * n_pairs]):
        a = a_ref[...]
        if precision is None:
            a = a.astype(_BF)
        d = jnp.dot(a, w_ref[...], preferred_element_type=_F32, precision=precision)
        acc = d if acc is None else acc + d
    o_ref[...] = acc


def _weight_spec(w, k, tn):
    if not isinstance(w, tuple):
        return w, pl.BlockSpec((k, tn), lambda i, j: (0, j))
    stack, layer, row0 = w
    assert row0 % k == 0
    return stack, pl.BlockSpec((None, k, tn), lambda i, j: (layer, row0 // k, j))


def _matmul(a_list, w_list, tm, tn, precision=None):
    n = a_list[0].shape[0]
    w0 = w_list[0]
    n_out = (w0[0] if isinstance(w0, tuple) else w0).shape[-1]
    tm = min(tm, n)
    tn = min(tn, n_out)
    views = [_weight_spec(w, a.shape[1], tn) for a, w in zip(a_list, w_list)]
    in_specs = [pl.BlockSpec((tm, a.shape[1]), lambda i, j: (i, 0)) for a in a_list]
    in_specs += [spec for _, spec in views]
    return pl.pallas_call(
        functools.partial(_mm_kernel, n_pairs=len(a_list), precision=precision),
        grid=(n // tm, n_out // tn),
        in_specs=in_specs,
        out_specs=pl.BlockSpec((tm, tn), lambda i, j: (i, j)),
        out_shape=jax.ShapeDtypeStruct((n, n_out), _F32),
        compiler_params=_params("parallel", "parallel"),
    )(*a_list, *[arr for arr, _ in views])


def _ln_kernel(x_ref, y_ref, g_ref, b_ref, o_ref, ob_ref, *, y_transposed):
    y = y_ref[...].T if y_transposed else y_ref[...]
    z = DN_ALPHA * x_ref[...] + y
    mu = jnp.mean(z, axis=-1, keepdims=True)
    zc = z - mu
    var = jnp.mean(zc * zc, axis=-1, keepdims=True)
    y = zc * lax.rsqrt(var + LN_EPS) * g_ref[...] + b_ref[...]
    o_ref[...] = y
    ob_ref[...] = y.astype(_BF)


def _add_layernorm(x, y, g, b, y_transposed=False):
    n, d = x.shape
    tm = min(128, n)
    row = pl.BlockSpec((tm, d), lambda i: (i, 0))
    vec = pl.BlockSpec((1, d), lambda i: (0, 0))
    y_spec = pl.BlockSpec((d, tm), lambda i: (0, i)) if y_transposed else row
    return pl.pallas_call(
        functools.partial(_ln_kernel, y_transposed=y_transposed),
        grid=(n // tm,),
        in_specs=[row, y_spec, vec, vec],
        out_specs=[row, row],
        out_shape=[jax.ShapeDtypeStruct((n, d), _F32), jax.ShapeDtypeStruct((n, d), _BF)],
        compiler_params=_params("parallel"),
    )(x, y, g.reshape(1, d), b.reshape(1, d))


def _log_sigmoid(x):
    return jnp.minimum(x, 0.0) - jnp.log1p(jnp.exp(-jnp.abs(x)))


def _mlstm_kernel(bif_ref, q_ref, k_ref, v_ref, og_ref, gi_ref, gf_ref, c0_ref, n0_ref, m0_ref,
                  tok_ref, c_ref, n_ref, m_ref, *, l_in, l):
    h = pl.program_id(1)

    @pl.when(pl.program_id(2) == 0)
    def _():
        c_ref[...] = c0_ref[...]
        n_ref[...] = n0_ref[...]
        m_ref[...] = m0_ref[...]

    def rows(ref):
        x = ref[...]
        if l_in == l:
            return x
        return jnp.concatenate([x, jnp.zeros((l - l_in, x.shape[1]), x.dtype)], axis=0)

    qf = rows(q_ref)
    kf = rows(k_ref) * (M_DQK ** -0.5)
    vb = rows(v_ref).astype(_BF)
    i_row = gi_ref[0, 0] + bif_ref[h]
    f_row = gf_ref[0, 0] + bif_ref[M_HEADS + h]
    lf_row = _log_sigmoid(f_row)

    t_idx = lax.broadcasted_iota(jnp.int32, (l, l), 0)
    s_idx = lax.broadcasted_iota(jnp.int32, (l, l), 1)
    causal = s_idx <= t_idx
    b_col = jnp.sum(jnp.where(causal, jnp.broadcast_to(lf_row, (l, l)), 0.0), axis=1, keepdims=True)
    b_row = _col_to_row(b_col, l)
    i_col = _row_to_col(i_row, l)

    m_prev = m_ref[0, 0, :, 0:1]
    dmat = jnp.where(causal, b_col - b_row + i_row, -jnp.inf)
    inter = b_col + m_prev
    m_t = jnp.maximum(inter, jnp.max(dmat, axis=1, keepdims=True))
    w_inter = jnp.exp(inter - m_t)

    qb = qf.astype(_BF)
    kb = kf.astype(_BF)
    s = lax.dot_general(qb, kb, _NT, preferred_element_type=_F32) * jnp.exp(dmat - m_t)
    c_old = c_ref[0, 0]
    n_old = n_ref[0, 0]
    num = (jnp.dot(s.astype(_BF), vb, preferred_element_type=_F32)
           + w_inter * jnp.dot(qb, c_old.astype(_BF), preferred_element_type=_F32))
    den = jnp.sum(s, axis=1, keepdims=True) + w_inter * jnp.sum(qf * n_old, axis=1, keepdims=True)
    hid = num / jnp.maximum(jnp.abs(den), jnp.exp(-m_t))
    tok = jax.nn.sigmoid(rows(og_ref)) * hid
    tok_ref[...] = tok[0:l_in]

    m_new = m_t[l - 1:l, :]
    b_last = b_col[l - 1:l, :]
    w_end = jnp.exp(b_last - b_col + i_col - m_new)
    decay = jnp.exp(b_last + m_prev - m_new)
    kw = kf * w_end
    c_ref[0, 0] = decay * c_old + lax.dot_general(kw.astype(_BF), vb, _TN, preferred_element_type=_F32)
    n_ref[0, 0] = decay * n_old + jnp.sum(kw, axis=0, keepdims=True)
    m_ref[0, 0] = jnp.broadcast_to(m_new, (1, 128))


def _mlstm(u, gates_t, b_if, c0, n0, m0, batch, seq):
    n = batch * seq
    l = min(seq, M_CHUNK_MAX)
    nc = seq // l
    l_pad = max(l, M_CHUNK_MIN)
    gi = gates_t[:M_HEADS].reshape(M_HEADS, n // l, 1, l)
    gf = gates_t[M_HEADS:].reshape(M_HEADS, n // l, 1, l)
    if l_pad != l:
        gi = jnp.pad(gi, ((0, 0), (0, 0), (0, 0), (0, l_pad - l)), constant_values=NEG_BIG)
        gf = jnp.pad(gf, ((0, 0), (0, 0), (0, 0), (0, l_pad - l)), constant_values=-NEG_BIG)
    vq = M_DV // M_DQK
    row = lambda b, h, c: b * nc + c
    state_c = pl.BlockSpec((1, 1, M_DQK, M_DV), lambda b, h, c: (b, h, 0, 0))
    state_n = pl.BlockSpec((1, 1, 1, M_DQK), lambda b, h, c: (b, h, 0, 0))
    state_m = pl.BlockSpec((1, 1, 1, 128), lambda b, h, c: (b, h, 0, 0))
    gate = pl.BlockSpec((1, 1, 1, l_pad), lambda b, h, c: (h, row(b, h, c), 0, 0))
    tok, c_new, n_new, m_new = pl.pallas_call(
        functools.partial(_mlstm_kernel, l_in=l, l=l_pad),
        grid=(batch, M_HEADS, nc),
        in_specs=[
            pl.BlockSpec(memory_space=pltpu.SMEM),
            pl.BlockSpec((l, M_DQK), lambda b, h, c: (row(b, h, c), h)),
            pl.BlockSpec((l, M_DQK), lambda b, h, c: (row(b, h, c), M_HEADS + h)),
            pl.BlockSpec((l, M_DV), lambda b, h, c: (row(b, h, c), M_HEADS + h)),
            pl.BlockSpec((l, M_DV), lambda b, h, c: (row(b, h, c), 2 * M_HEADS + h)),
            gate, gate, state_c, state_n, state_m,
        ],
        out_specs=[pl.BlockSpec((l, M_DV), lambda b, h, c: (row(b, h, c), h)), state_c, state_n, state_m],
        out_shape=[
            jax.ShapeDtypeStruct((n, MIX_TOKEN), _F32),
            jax.ShapeDtypeStruct((batch, M_HEADS, M_DQK, M_DV), _F32),
            jax.ShapeDtypeStruct((batch, M_HEADS, 1, M_DQK), _F32),
            jax.ShapeDtypeStruct((batch, M_HEADS, 1, 128), _F32),
        ],
        compiler_params=_params("parallel", "parallel", "arbitrary"),
    )(b_if, u, u, u, u, gi, gf, c0, n0.reshape(batch, M_HEADS, 1, M_DQK),
      jnp.broadcast_to(m0[:, :, None, None], (batch, M_HEADS, 1, 128)))
    assert vq * M_DQK == M_DV
    return tok, (c_new, n_new.reshape(batch, M_HEADS, M_DQK), m_new[:, :, 0, 0])


def _memattn_kernel(q_ref, k_ref, v_ref, o_ref):
    q = q_ref[...].astype(_BF)
    logits = lax.dot_general(q, k_ref[0].astype(_BF), _NT, preferred_element_type=_F32) * (MEM_DH ** -0.5)
    e = jnp.exp(logits - jnp.max(logits, axis=-1, keepdims=True))
    p = e / jnp.sum(e, axis=-1, keepdims=True)
    o_ref[...] = jnp.dot(p.astype(_BF), v_ref[0].astype(_BF), preferred_element_type=_F32)


def _mem_attend(qsrc, q_col0, mk, mv, k_col0, v_col0, batch, seq):
    n = batch * seq
    tq = min(seq, 512)
    nq = seq // tq
    qc, kc, vc = q_col0 // MEM_DH, k_col0 // MEM_DH, v_col0 // MEM_DH
    return pl.pallas_call(
        _memattn_kernel,
        grid=(batch, nq, MEM_HEADS),
        in_specs=[
            pl.BlockSpec((tq, MEM_DH), lambda b, t, h: (b * nq + t, qc + h)),
            pl.BlockSpec((1, MEM_SLOTS, MEM_DH), lambda b, t, h: (b, 0, kc + h)),
            pl.BlockSpec((1, MEM_SLOTS, MEM_DH), lambda b, t, h: (b, 0, vc + h)),
        ],
        out_specs=pl.BlockSpec((tq, MEM_DH), lambda b, t, h: (b * nq + t, h)),
        out_shape=jax.ShapeDtypeStruct((n, MIX_MEM), _F32),
        compiler_params=_params("parallel", "parallel", "parallel"),
    )(qsrc, mk, mv)


def _rel_bucket(dist):
    n = jnp.maximum(dist, 0)
    max_exact = REL_BUCKETS // 2
    nf = jnp.maximum(n, 1).astype(_F32)
    large = max_exact + (jnp.log(nf / max_exact) / math.log(REL_MAX_DIST / max_exact)
                         * (REL_BUCKETS - max_exact)).astype(jnp.int32)
    return jnp.where(n < max_exact, n, jnp.minimum(large, REL_BUCKETS - 1))


def _bias_lookup(rb_ref, head, dist):
    bucket = _rel_bucket(dist)
    val = jnp.full(dist.shape, rb_ref[0, head], _F32)
    for b in range(1, REL_BUCKETS):
        val = jnp.where(bucket == b, rb_ref[b, head], val)
    return val


def _bias_tiles_kernel(rb_ref, o_ref):
    head, delta = pl.program_id(0), pl.program_id(1)
    key = lax.broadcasted_iota(jnp.int32, (MOBA_BLOCK, MOBA_BLOCK), 0)
    qry = lax.broadcasted_iota(jnp.int32, (MOBA_BLOCK, MOBA_BLOCK), 1)
    o_ref[0, 0] = _bias_lookup(rb_ref, head, delta * MOBA_BLOCK + qry - key)


def _bias_tiles(rel_bias, nb):
    return pl.pallas_call(
        _bias_tiles_kernel,
        grid=(B_HEADS, nb),
        in_specs=[pl.BlockSpec(memory_space=pltpu.SMEM)],
        out_specs=pl.BlockSpec((1, 1, MOBA_BLOCK, MOBA_BLOCK), lambda h, d: (h, d, 0, 0)),
        out_shape=jax.ShapeDtypeStruct((B_HEADS, nb, MOBA_BLOCK, MOBA_BLOCK), _F32),
        compiler_params=_params("parallel", "parallel"),
    )(rel_bias)


def _moba_prompt_kernel(q_ref, k_ref, v_ref, bias_ref, o_ref, s_scr, *, nb):
    blk = MOBA_BLOCK
    scale = B_DH ** -0.5
    means = jnp.concatenate(
        [jnp.mean(k_ref[c * blk:(c + 1) * blk, :], axis=0, keepdims=True) for c in range(nb)], axis=0)
    gate = lax.dot_general(means, q_ref[...], _NT, preferred_element_type=_F32,
                           precision=lax.Precision.HIGHEST)
    key = lax.broadcasted_iota(jnp.int32, (blk, blk), 0)
    qry = lax.broadcasted_iota(jnp.int32, (blk, blk), 1)
    causal_pen = jnp.where(key <= qry, 0.0, -jnp.inf)
    kb = [k_ref[c * blk:(c + 1) * blk, :].astype(_BF) for c in range(nb)]
    vb = [v_ref[c * blk:(c + 1) * blk, :].astype(_BF) for c in range(nb)]

    slot = 0
    for j in range(nb):
        qb = q_ref[j * blk:(j + 1) * blk, :].astype(_BF)
        g = [gate[c:c + 1, j * blk:(j + 1) * blk] for c in range(j)]
        pens = []
        for c in range(j):
            if j <= MOBA_TOPK:
                pens.append(None)
                continue
            rank = jnp.zeros((1, blk), _F32)
            for c2 in range(j):
                if c2 != c:
                    rank = rank + jnp.where(g[c2] >= g[c] if c2 < c else g[c2] > g[c], 1.0, 0.0)
            pens.append(jnp.where(rank < MOBA_TOPK, 0.0, -jnp.inf))

        m_run = jnp.full((1, blk), -jnp.inf, _F32)
        for c in range(j + 1):
            st = lax.dot_general(kb[c], qb, _NT, preferred_element_type=_F32) * scale + bias_ref[0, j - c]
            if c == j:
                st = st + causal_pen
            elif pens[c] is not None:
                st = st + pens[c]
            s_scr[slot + c] = st
            m_run = jnp.maximum(m_run, jnp.max(st, axis=0, keepdims=True))
        l_run = jnp.zeros((1, blk), _F32)
        acc = jnp.zeros((blk, B_DH), _F32)
        for c in range(j + 1):
            p = jnp.exp(s_scr[slot + c] - m_run)
            l_run = l_run + jnp.sum(p, axis=0, keepdims=True)
            acc = acc + lax.dot_general(p.astype(_BF), vb[c], _TN, preferred_element_type=_F32)
        o_ref[j * blk:(j + 1) * blk, :] = acc / _row_to_col(l_run, blk)
        slot += j + 1


def _moba_prompt(qsrc, k, v, bias, batch, seq):
    n = batch * seq
    nb = seq // MOBA_BLOCK
    per_head = pl.BlockSpec((seq, B_DH), lambda h, b: (b, h))
    return pl.pallas_call(
        functools.partial(_moba_prompt_kernel, nb=nb),
        grid=(B_HEADS, batch),
        in_specs=[per_head, per_head, per_head,
                  pl.BlockSpec((1, nb, MOBA_BLOCK, MOBA_BLOCK), lambda h, b: (h, 0, 0, 0))],
        out_specs=per_head,
        out_shape=jax.ShapeDtypeStruct((n, MIX_TOKEN), _F32),
        scratch_shapes=[pltpu.VMEM((nb * (nb + 1) // 2, MOBA_BLOCK, MOBA_BLOCK), _F32)],
        compiler_params=_params("parallel", "parallel"),
    )(qsrc, k, v, bias)


def _page_means_kernel(pt_ref, ka_ref, kb_ref, o_ref):
    o_ref[0, 0] = (jnp.sum(ka_ref[0], axis=0) + jnp.sum(kb_ref[0], axis=0)) / MOBA_BLOCK


def _page_means(k_pool, page_table, nfull):
    batch = page_table.shape[0]
    assert MOBA_BLOCK == 2 * PAGE_SIZE
    page = lambda which: pl.BlockSpec((1, PAGE_SIZE, B_HEADS, B_DH), lambda b, c, pt: (pt[b, 2 * c + which], 0, 0, 0))
    grid_spec = pltpu.PrefetchScalarGridSpec(
        num_scalar_prefetch=1,
        grid=(batch, nfull),
        in_specs=[page(0), page(1)],
        out_specs=pl.BlockSpec((1, 1, B_HEADS, B_DH), lambda b, c, pt: (b, c, 0, 0)),
    )
    return pl.pallas_call(
        _page_means_kernel,
        grid_spec=grid_spec,
        out_shape=jax.ShapeDtypeStruct((batch, nfull, B_HEADS, B_DH), _F32),
        compiler_params=_params("parallel", "parallel"),
    )(page_table, k_pool, k_pool)


def _moba_sample_select_kernel(q_ref, means_ref, sel_ref, *, seq, nfull):
    lanes = 128
    col = lax.broadcasted_iota(jnp.int32, (seq, lanes), 1)
    colf = col.astype(_F32)
    out = jnp.zeros((seq, lanes), _F32)
    for h in range(B_HEADS):
        qh = q_ref[:, h * B_DH:(h + 1) * B_DH]
        mh = means_ref[0, :, h, :]
        mh = jnp.concatenate([mh, jnp.zeros((lanes - nfull, B_DH), _F32)], axis=0)
        gate = lax.dot_general(qh, mh, _NT, preferred_element_type=_F32, precision=lax.Precision.HIGHEST)
        gate = jnp.where(col < nfull, gate, -jnp.inf)
        for kk in range(MOBA_TOPK):
            best = jnp.max(gate, axis=1, keepdims=True)
            idx = jnp.min(jnp.where(gate == best, colf, float(lanes)), axis=1, keepdims=True)
            out = jnp.where(col == h * MOBA_TOPK + kk, idx, out)
            gate = jnp.where(colf == idx, -jnp.inf, gate)
    sel_ref[0] = out.astype(jnp.int32)


def _moba_sample_select(qsrc, means, batch, seq):
    nfull = means.shape[1]
    assert nfull >= MOBA_TOPK and nfull <= 128 and B_HEADS * MOBA_TOPK <= 128
    return pl.pallas_call(
        functools.partial(_moba_sample_select_kernel, seq=seq, nfull=nfull),
        grid=(batch,),
        in_specs=[
            pl.BlockSpec((seq, MIX_TOKEN), lambda b: (b, 0)),
            pl.BlockSpec((1, nfull, B_HEADS, B_DH), lambda b: (b, 0, 0, 0)),
        ],
        out_specs=pl.BlockSpec((1, seq, 128), lambda b: (b, 0, 0)),
        out_shape=jax.ShapeDtypeStruct((batch, seq, 128), jnp.int32),
        compiler_params=_params("parallel"),
    )(qsrc, means)


def _moba_sample_kernel(sel_ref, pt_ref, q_ref, kn_ref, vn_ref, rb_ref, kpool_ref, vpool_ref, o_ref,
                        kbuf, vbuf, sems, *, seq, past):
    b, h = pl.program_id(0), pl.program_id(1)
    blk = MOBA_BLOCK
    ppb = blk // PAGE_SIZE
    nsel = seq * MOBA_TOPK

    def copies(t, kk, pg):
        block = sel_ref[b, t, h * MOBA_TOPK + kk]
        page = pt_ref[b, block * ppb + pg]
        dst = pl.ds(((t * MOBA_TOPK + kk) * ppb + pg) * PAGE_SIZE, PAGE_SIZE)
        return (pltpu.make_async_copy(kpool_ref.at[page, :, h, :], kbuf.at[dst, :], sems.at[0]),
                pltpu.make_async_copy(vpool_ref.at[page, :, h, :], vbuf.at[dst, :], sems.at[1]))

    every = [(t, kk, pg) for t in range(seq) for kk in range(MOBA_TOPK) for pg in range(ppb)]
    for idx in every:
        for cp in copies(*idx):
            cp.start()

    qf = q_ref[...]
    qb = qf.astype(_BF)
    scale = B_DH ** -0.5
    pad = jnp.zeros((128 - seq, B_DH), _F32)
    k_own = jnp.concatenate([kn_ref[...], pad], axis=0).astype(_BF)
    v_own = jnp.concatenate([vn_ref[...], pad], axis=0).astype(_BF)
    t_own = lax.broadcasted_iota(jnp.int32, (seq, 128), 0)
    s_own = lax.broadcasted_iota(jnp.int32, (seq, 128), 1)
    lo = (lax.dot_general(qb, k_own, _NT, preferred_element_type=_F32) * scale
          + _bias_lookup(rb_ref, h, t_own - s_own))
    lo = jnp.where(s_own <= t_own, lo, -jnp.inf)

    ncol = nsel * blk
    t_sel = lax.broadcasted_iota(jnp.int32, (seq, ncol), 0)
    col = lax.broadcasted_iota(jnp.int32, (seq, ncol), 1)
    slot = col >> int(math.log2(blk))
    block_of_col = jnp.zeros((seq, ncol), jnp.int32)
    token_of_col = jnp.zeros((seq, ncol), jnp.int32)
    for t in range(seq):
        for kk in range(MOBA_TOPK):
            here = slot == t * MOBA_TOPK + kk
            block_of_col = jnp.where(here, sel_ref[b, t, h * MOBA_TOPK + kk], block_of_col)
            token_of_col = jnp.where(here, t, token_of_col)
    dist = past + t_sel - (block_of_col * blk + (col & (blk - 1)))
    bias_sel = _bias_lookup(rb_ref, h, dist)

    for idx in every:
        for cp in copies(*idx):
            cp.wait()

    ls = lax.dot_general(qb, kbuf[...].astype(_BF), _NT, preferred_element_type=_F32) * scale + bias_sel
    ls = jnp.where(token_of_col == t_sel, ls, -jnp.inf)
    m = jnp.maximum(jnp.max(lo, axis=1, keepdims=True), jnp.max(ls, axis=1, keepdims=True))
    p_own = jnp.exp(lo - m)
    p_sel = jnp.exp(ls - m)
    denom = jnp.sum(p_own, axis=1, keepdims=True) + jnp.sum(p_sel, axis=1, keepdims=True)
    out = (jnp.dot(p_own.astype(_BF), v_own, preferred_element_type=_F32)
           + jnp.dot(p_sel.astype(_BF), vbuf[...].astype(_BF), preferred_element_type=_F32))
    o_ref[...] = out / denom


def _moba_sample(qsrc, k_new, v_new, sel, page_table, rel_bias, k_pool, v_pool, batch, seq):
    n = batch * seq
    past = page_table.shape[1] * PAGE_SIZE
    assert past % MOBA_BLOCK == 0 and (past + seq - 1) // MOBA_BLOCK == past // MOBA_BLOCK
    rows = seq * MOBA_TOPK * MOBA_BLOCK
    grid_spec = pltpu.PrefetchScalarGridSpec(
        num_scalar_prefetch=2,
        grid=(batch, B_HEADS),
        in_specs=[
            pl.BlockSpec((seq, B_DH), lambda b, h, *_: (b, h)),
            pl.BlockSpec((seq, B_DH), lambda b, h, *_: (b, h)),
            pl.BlockSpec((seq, B_DH), lambda b, h, *_: (b, h)),
            pl.BlockSpec(memory_space=pltpu.SMEM),
            pl.BlockSpec(memory_space=pl.ANY),
            pl.BlockSpec(memory_space=pl.ANY),
        ],
        out_specs=pl.BlockSpec((seq, B_DH), lambda b, h, *_: (b, h)),
        scratch_shapes=[pltpu.VMEM((rows, B_DH), _F32), pltpu.VMEM((rows, B_DH), _F32),
                        pltpu.SemaphoreType.DMA((2,))],
    )
    return pl.pallas_call(
        functools.partial(_moba_sample_kernel, seq=seq, past=past),
        grid_spec=grid_spec,
        out_shape=jax.ShapeDtypeStruct((n, MIX_TOKEN), _F32),
        compiler_params=_params("arbitrary", "arbitrary"),
    )(sel, page_table, qsrc, k_new, v_new, rel_bias, k_pool, v_pool)


def _top_sorted(scores, count):
    tm = scores.shape[1]
    rank = lax.broadcasted_iota(jnp.int32, (count, tm), 0)
    vals = []
    stacked = jnp.zeros((count, tm), _F32)
    cur = scores
    for r in range(count):
        best = jnp.max(cur, axis=0, keepdims=True)
        vals.append(best)
        stacked = jnp.where(rank == r, best, stacked)
        cur = jnp.where(cur >= best, -jnp.inf, cur)
    return vals, stacked


def _peer_candidates(a, a_st, b, b_st):
    tm = a_st.shape[1]
    half = PEER_TOPK // 2
    groups = [a[0] + b_st[0:half], a[0] + b_st[half:PEER_TOPK]]
    groups += [a[i] + b_st[0:half] for i in range(1, half)]
    groups.append(a_st[half:PEER_TOPK] + b[0])
    return jnp.concatenate([jnp.broadcast_to(g, (half, tm)) for g in groups], axis=0)


def _peer_select_kernel(q_ref, keys_ref, s1_ref, s2_ref, e1_ref, e2_ref, tau_ref):
    tm = q_ref.shape[0]
    for h in range(PEER_HEADS):
        qh = q_ref[:, h * PEER_DKEY:(h + 1) * PEER_DKEY]
        st = lax.dot_general(keys_ref[h], qh, _NT, preferred_element_type=_F32,
                             precision=lax.Precision.HIGHEST)
        s1 = st[0:PEER_NKEYS]
        s2 = st[PEER_NKEYS:2 * PEER_NKEYS]
        a, a_st = _top_sorted(s1, PEER_TOPK)
        bb, b_st = _top_sorted(s2, PEER_TOPK)
        cand = _peer_candidates(a, a_st, bb, b_st)
        tau = jnp.zeros((1, tm), _F32)
        taken = jnp.zeros((1, tm), _F32)
        cur = cand
        for _ in range(PEER_TOPK):
            best = jnp.max(cur, axis=0, keepdims=True)
            hit = cur == best
            tau = jnp.where(taken < PEER_TOPK, best, tau)
            taken = taken + jnp.sum(jnp.where(hit, 1.0, 0.0), axis=0, keepdims=True)
            cur = jnp.where(hit, -jnp.inf, cur)
        top = a[0] + bb[0]
        z = jnp.sum(jnp.where(cand >= tau, jnp.exp(cand - top), 0.0), axis=0, keepdims=True)
        s1_ref[h] = s1
        s2_ref[h] = s2
        e1_ref[h] = jnp.exp(s1 - a[0]) / z
        e2_ref[h] = jnp.exp(s2 - bb[0])
        tau_ref[h] = jnp.broadcast_to(tau, (8, tm))


def _peer_select(q, keys_bd):
    n = q.shape[0]
    tm = min(256, n)
    half = pl.BlockSpec((PEER_HEADS, PEER_NKEYS, tm), lambda i: (0, 0, i))
    half_shape = jax.ShapeDtypeStruct((PEER_HEADS, PEER_NKEYS, n), _F32)
    return pl.pallas_call(
        _peer_select_kernel,
        grid=(n // tm,),
        in_specs=[pl.BlockSpec((tm, PEER_HEADS * PEER_DKEY), lambda i: (i, 0)),
                  pl.BlockSpec((PEER_HEADS, 2 * PEER_NKEYS, PEER_DKEY), lambda i: (0, 0, 0))],
        out_specs=[half, half, half, half, pl.BlockSpec((PEER_HEADS, 8, tm), lambda i: (0, 0, i))],
        out_shape=[half_shape, half_shape, half_shape, half_shape,
                   jax.ShapeDtypeStruct((PEER_HEADS, 8, n), _F32)],
        compiler_params=_params("parallel"),
    )(q, keys_bd)


PEER_TE = 1024
PEER_TM = 512
PEER_TM_GROUP = 512
PEER_SPLIT = 2
PEER_D_CHUNK = 1024


def _peer_dense_kernel(x_ref, u_ref, vt_ref, s1_ref, s2_ref, e1_ref, e2_ref, tau_ref, o_ref, *, te, group):
    @pl.when(pl.program_id(1) == 0)
    def _():
        o_ref[...] = jnp.zeros_like(o_ref)

    tm = x_ref.shape[0]
    sub = te // PEER_SPLIT
    rows = sub // PEER_NKEYS
    for g0 in range(0, tm, group):
        cols = slice(g0, g0 + group)
        xg = x_ref[cols, :]
        a_t = [lax.dot_general(u_ref[s * sub:(s + 1) * sub, :], xg, _NT, preferred_element_type=_F32)
               for s in range(PEER_SPLIT)]
        for s in range(PEER_SPLIT):
            parts = []
            for r in range(rows):
                k1 = s * rows + r
                w = jnp.zeros((PEER_NKEYS, group), _F32)
                for h in range(PEER_HEADS):
                    total = s1_ref[h, k1:k1 + 1, cols] + s2_ref[h, :, cols]
                    w = w + jnp.where(total >= tau_ref[h, 0:1, cols],
                                      e2_ref[h, :, cols] * e1_ref[h, k1:k1 + 1, cols], 0.0)
                act = jax.nn.gelu(a_t[s][r * PEER_NKEYS:(r + 1) * PEER_NKEYS])
                parts.append((act * w).astype(_BF))
            gated = jnp.concatenate(parts, axis=0)
            for d0 in range(0, o_ref.shape[0], PEER_D_CHUNK):
                o_ref[d0:d0 + PEER_D_CHUNK, cols] += jnp.dot(
                    vt_ref[d0:d0 + PEER_D_CHUNK, s * sub:(s + 1) * sub], gated, preferred_element_type=_F32)


def _peer_dense(xb, u_stack, vt_stack, layer, sel):
    n, d = xb.shape
    tm = min(PEER_TM, n)
    te = PEER_TE
    rows = te // PEER_NKEYS
    assert rows == 8
    held = pl.Buffered(1)
    per_key2 = pl.BlockSpec((PEER_HEADS, PEER_NKEYS, tm), lambda i, e: (0, 0, i), pipeline_mode=held)
    per_key1 = pl.BlockSpec((PEER_HEADS, rows, tm), lambda i, e: (0, e, i))
    s1, s2, e1, e2, tau = sel
    return pl.pallas_call(
        functools.partial(_peer_dense_kernel, te=te, group=min(PEER_TM_GROUP, tm)),
        grid=(n // tm, PEER_N // te),
        in_specs=[pl.BlockSpec((tm, d), lambda i, e: (i, 0), pipeline_mode=held),
                  pl.BlockSpec((None, te, d), lambda i, e: (layer, e, 0)),
                  pl.BlockSpec((None, d, te), lambda i, e: (layer, 0, e)),
                  per_key1, per_key2, per_key1, per_key2,
                  pl.BlockSpec((PEER_HEADS, 8, tm), lambda i, e: (0, 0, i), pipeline_mode=held)],
        out_specs=pl.BlockSpec((d, tm), lambda i, e: (0, i), pipeline_mode=held),
        out_shape=jax.ShapeDtypeStruct((d, n), _F32),
        compiler_params=_params("parallel", "arbitrary"),
    )(xb, u_stack, vt_stack, s1, s2, e1, e2, tau)


def _peer(xb, layer, wq_stack, keys_bd, u_stack, vt_stack):
    n = xb.shape[0]
    n_pad = max(n, 128)
    if n_pad != n:
        xb = jnp.pad(xb, ((0, n_pad - n), (0, 0)))
    q = _matmul([xb], [(wq_stack, layer, 0)], 1024, 512)
    return _peer_dense(xb, u_stack, vt_stack, layer, _peer_select(q, keys_bd))


def _layer_tail(x, tok, mem, layer, w):
    n = x.shape[0]
    mixed = _matmul([tok, mem], [(w['out'], layer, 0), (w['out'], layer, MIX_TOKEN)], 512, 512)
    x1, x1b = _add_layernorm(x, mixed, *w['ln1'][layer])
    ffn_t = _peer(x1b, layer, w['peer_wq'], w['peer_keys'][layer], w['peer_u'], w['peer_vt'])
    if ffn_t.shape[1] == n:
        return _add_layernorm(x1, ffn_t, *w['ln2'][layer], y_transposed=True)
    return _add_layernorm(x1, ffn_t[:, :n].T, *w['ln2'][layer])


def _trunk(x, batch, seq, mem_kv, mlstm_init, moba_attend, w):
    xb = x.astype(_BF)
    u = _matmul([xb], [w['in_a_main']], 1024, 512)
    qm = _matmul([xb], [w['in_a_mem']], 1024, 512)
    gates = _matmul([x], [w['in_a_gates']], 512, 128, precision=lax.Precision.HIGHEST)
    tok, state = _mlstm(u, gates[:, :2 * M_HEADS].T, w['b_if'], *mlstm_init, batch, seq)
    mem = _mem_attend(qm, 0, *mem_kv[0], batch, seq)
    x, xb = _layer_tail(x, tok, mem, 0, w)
    k_sh = _matmul([xb], [w['k_shared']], 1024, 512)
    v_sh = _matmul([xb], [w['v_shared']], 1024, 512)
    ub = _matmul([xb], [w['in_b']], 1024, 512)
    tok = moba_attend(ub, k_sh, v_sh)
    mem = _mem_attend(ub, MIX_TOKEN, *mem_kv[1], batch, seq)
    x, _ = _layer_tail(x, tok, mem, 1, w)
    return x, state, k_sh, v_sh


def kernel(x_prompt, x_sample, cache_moba_k, cache_moba_v, cache_mem_k, cache_mem_v, state_mlstm_c,
           state_mlstm_n, state_mlstm_m, page_table, mem_prompt, w_in_a, b_if_a, w_in_b, w_kv_shared, rel_bias,
           w_mem_kv, w_out, ln1_g, ln1_b, ln2_g, ln2_b, peer_wq, peer_keys, peer_u, peer_v):
    bp, tp, d = x_prompt.shape
    bs, ts, _ = x_sample.shape
    gate0 = 2 * M_HEADS * M_DQK + 2 * MIX_TOKEN
    wa = w_in_a[0]
    zeros = jnp.zeros((PEER_HEADS, PEER_NKEYS, PEER_DKEY // 2), _F32)

    def keys_blockdiag(keys):
        return jnp.concatenate([jnp.concatenate([keys[:, 0], zeros], axis=2),
                                jnp.concatenate([zeros, keys[:, 1]], axis=2)], axis=1)

    w = {
        'in_a_main': wa[:, :gate0].astype(_BF),
        'in_a_mem': wa[:, gate0 + 2 * M_HEADS:].astype(_BF),
        'in_a_gates': jnp.pad(wa[:, gate0:gate0 + 2 * M_HEADS], ((0, 0), (0, 128 - 2 * M_HEADS))),
        'b_if': b_if_a[0],
        'in_b': w_in_b[0].astype(_BF),
        'k_shared': w_kv_shared[:, :MIX_TOKEN].astype(_BF),
        'v_shared': w_kv_shared[:, MIX_TOKEN:].astype(_BF),
        'out': w_out.astype(_BF),
        'ln1': [(ln1_g[l], ln1_b[l]) for l in range(DEPTH)],
        'ln2': [(ln2_g[l], ln2_b[l]) for l in range(DEPTH)],
        'peer_wq': peer_wq.astype(_BF),
        'peer_keys': [keys_blockdiag(peer_keys[l]) for l in range(DEPTH)],
        'peer_u': peer_u.astype(_BF),
        'peer_vt': jnp.swapaxes(peer_v, 1, 2).astype(_BF),
    }

    mem_w = jnp.concatenate([w_mem_kv[l] for l in range(DEPTH)], axis=1).astype(_BF)
    mkv = _matmul([mem_prompt.reshape(bp * MEM_SLOTS, d).astype(_BF)], [mem_w], 1024, 512)
    mkv3 = mkv.reshape(bp, MEM_SLOTS, 2 * DEPTH * MIX_MEM)
    mem_kv_p = [(mkv3, mkv3, 2 * l * MIX_MEM, (2 * l + 1) * MIX_MEM) for l in range(DEPTH)]
    init_p = (jnp.zeros((bp, M_HEADS, M_DQK, M_DV), _F32), jnp.zeros((bp, M_HEADS, M_DQK), _F32),
              jnp.zeros((bp, M_HEADS), _F32))
    bias = _bias_tiles(rel_bias, tp // MOBA_BLOCK)
    y_p, st_p, k_p, v_p = _trunk(
        x_prompt.reshape(bp * tp, d), bp, tp, mem_kv_p, init_p,
        lambda ub, k_sh, v_sh: _moba_prompt(ub, k_sh, v_sh, bias, bp, tp), w)

    mem_kv_s = [(cache_mem_k[l].reshape(bs, MEM_SLOTS, MIX_MEM), cache_mem_v[l].reshape(bs, MEM_SLOTS, MIX_MEM), 0, 0)
                for l in range(DEPTH)]
    init_s = (state_mlstm_c[0], state_mlstm_n[0], state_mlstm_m[0])
    nfull = page_table.shape[1] * PAGE_SIZE // MOBA_BLOCK

    def moba_sample(ub, k_sh, v_sh):
        means = _page_means(cache_moba_k, page_table, nfull)
        sel = _moba_sample_select(ub, means, bs, ts)
        return _moba_sample(ub, k_sh, v_sh, sel, page_table, rel_bias, cache_moba_k, cache_moba_v, bs, ts)

    y_s, st_s, k_s, v_s = _trunk(x_sample.reshape(bs * ts, d), bs, ts, mem_kv_s, init_s, moba_sample, w)

    mem_k_p = jnp.stack([mkv3[:, :, 2 * l * MIX_MEM:(2 * l + 1) * MIX_MEM] for l in range(DEPTH)])
    mem_v_p = jnp.stack([mkv3[:, :, (2 * l + 1) * MIX_MEM:(2 * l + 2) * MIX_MEM] for l in range(DEPTH)])
    kv_shape = (bp, MEM_SLOTS, MEM_HEADS, MEM_DH)
    return (y_p.reshape(bp, tp, d), y_s.reshape(bs, ts, d),
            st_p[0][None], st_p[1][None], st_p[2][None],
            k_p.reshape(bp, tp, B_HEADS, B_DH), v_p.reshape(bp, tp, B_HEADS, B_DH),
            mem_k_p.reshape((DEPTH,) + kv_shape), mem_v_p.reshape((DEPTH,) + kv_shape),
            st_s[0][None], st_s[1][None], st_s[2][None],
            k_s.reshape(bs, ts, B_HEADS, B_DH), v_s.reshape(bs, ts, B_HEADS, B_DH))
```

```python
import functools
import math

import jax
import jax.numpy as jnp
from jax import lax
from jax.experimental import pallas as pl
from jax.experimental.pallas import tpu as pltpu

D_MODEL = 4096
DEPTH = 2
PAGE_SIZE = 128
MIX_TOKEN = 3 * D_MODEL // 4
MIX_MEM = D_MODEL // 4
M_HEADS = 6
M_DV = MIX_TOKEN // M_HEADS
M_DQK = M_DV // 2
M_CHUNK_MAX = 256
M_CHUNK_MIN = 128
B_HEADS = 24
B_DH = MIX_TOKEN // B_HEADS
MOBA_BLOCK = 256
MOBA_TOPK = 3
MEM_SLOTS = 256
MEM_HEADS = 4
MEM_DH = MIX_MEM // MEM_HEADS
REL_BUCKETS = 32
REL_MAX_DIST = 4096
PEER_HEADS = 8
PEER_NKEYS = 128
PEER_N = PEER_NKEYS * PEER_NKEYS
PEER_DKEY = 128
PEER_TOPK = 16
DN_ALPHA = (2.0 * DEPTH) ** 0.25
LN_EPS = 1e-5

VMEM_LIMIT_V7X = 56 * 1024 * 1024
NEG_BIG = -1e30

_NT = (((1,), (1,)), ((), ()))
_TN = (((0,), (0,)), ((), ()))
_BF = jnp.bfloat16
_F32 = jnp.float32


def _params(*sem):
    return pltpu.CompilerParams(dimension_semantics=sem, vmem_limit_bytes=VMEM_LIMIT_V7X)


def _row_to_col(row, n):
    eye = lax.broadcasted_iota(jnp.int32, (n, n), 0) == lax.broadcasted_iota(jnp.int32, (n, n), 1)
    return jnp.sum(jnp.where(eye, jnp.broadcast_to(row, (n, n)), 0.0), axis=1, keepdims=True)


def _col_to_row(col, n):
    eye = lax.broadcasted_iota(jnp.int32, (n, n), 0) == lax.broadcasted_iota(jnp.int32, (n, n), 1)
    return jnp.sum(jnp.where(eye, jnp.broadcast_to(col, (n, n)), 0.0), axis=0, keepdims=True)


def _mm_kernel(*refs, n_pairs, precision):
    o_ref = refs[2 * n_pairs]
    acc = None
    for a_ref, w_ref in zip(refs[:n_pairs], refs[n_pairs:2 * n_pairs]):
        a = a_ref[...]
        if precision is None:
            a = a.astype(_BF)
        d = jnp.dot(a, w_ref[...], preferred_element_type=_F32, precision=precision)
        acc = d if acc is None else acc + d
    o_ref[...] = acc


def _weight_spec(w, k, tn):
    if not isinstance(w, tuple):
        return w, pl.BlockSpec((k, tn), lambda i, j: (0, j))
    stack, layer, row0 = w
    assert row0 % k == 0
    return stack, pl.BlockSpec((None, k, tn), lambda i, j: (layer, row0 // k, j))


def _matmul(a_list, w_list, tm, tn, precision=None):
    n = a_list[0].shape[0]
    w0 = w_list[0]
    n_out = (w0[0] if isinstance(w0, tuple) else w0).shape[-1]
    tm = min(tm, n)
    tn = min(tn, n_out)
    views = [_weight_spec(w, a.shape[1], tn) for a, w in zip(a_list, w_list)]
    in_specs = [pl.BlockSpec((tm, a.shape[1]), lambda i, j: (i, 0)) for a in a_list]
    in_specs += [spec for _, spec in views]
    return pl.pallas_call(
        functools.partial(_mm_kernel, n_pairs=len(a_list), precision=precision),
        grid=(n // tm, n_out // tn),
        in_specs=in_specs,
        out_specs=pl.BlockSpec((tm, tn), lambda i, j: (i, j)),
        out_shape=jax.ShapeDtypeStruct((n, n_out), _F32),
        compiler_params=_params("parallel", "parallel"),
    )(*a_list, *[arr for arr, _ in views])


def _ln_kernel(x_ref, y_ref, g_ref, b_ref, o_ref, ob_ref, *, y_transposed):
    y = y_ref[...].T if y_transposed else y_ref[...]
    z = DN_ALPHA * x_ref[...] + y
    mu = jnp.mean(z, axis=-1, keepdims=True)
    zc = z - mu
    var = jnp.mean(zc * zc, axis=-1, keepdims=True)
    y = zc * lax.rsqrt(var + LN_EPS) * g_ref[...] + b_ref[...]
    o_ref[...] = y
    ob_ref[...] = y.astype(_BF)


def _add_layernorm(x, y, g, b, y_transposed=False):
    n, d = x.shape
    tm = min(128, n)
    row = pl.BlockSpec((tm, d), lambda i: (i, 0))
    vec = pl.BlockSpec((1, d), lambda i: (0, 0))
    y_spec = pl.BlockSpec((d, tm), lambda i: (0, i)) if y_transposed else row
    return pl.pallas_call(
        functools.partial(_ln_kernel, y_transposed=y_transposed),
        grid=(n // tm,),
        in_specs=[row, y_spec, vec, vec],
        out_specs=[row, row],
        out_shape=[jax.ShapeDtypeStruct((n, d), _F32), jax.ShapeDtypeStruct((n, d), _BF)],
        compiler_params=_params("parallel"),
    )(x, y, g.reshape(1, d), b.reshape(1, d))


def _log_sigmoid(x):
    return jnp.minimum(x, 0.0) - jnp.log1p(jnp.exp(-jnp.abs(x)))


def _mlstm_kernel(bif_ref, q_ref, k_ref, v_ref, og_ref, gi_ref, gf_ref, c0_ref, n0_ref, m0_ref,
                  tok_ref, c_ref, n_ref, m_ref, *, l_in, l):
    h = pl.program_id(1)

    @pl.when(pl.program_id(2) == 0)
    def _():
        c_ref[...] = c0_ref[...]
        n_ref[...] = n0_ref[...]
        m_ref[...] = m0_ref[...]

    def rows(ref):
        x = ref[...]
        if l_in == l:
            return x
        return jnp.concatenate([x, jnp.zeros((l - l_in, x.shape[1]), x.dtype)], axis=0)

    qf = rows(q_ref)
    kf = rows(k_ref) * (M_DQK ** -0.5)
    vb = rows(v_ref).astype(_BF)
    i_row = gi_ref[0, 0] + bif_ref[h]
    f_row = gf_ref[0, 0] + bif_ref[M_HEADS + h]
    lf_row = _log_sigmoid(f_row)

    t_idx = lax.broadcasted_iota(jnp.int32, (l, l), 0)
    s_idx = lax.broadcasted_iota(jnp.int32, (l, l), 1)
    causal = s_idx <= t_idx
    b_col = jnp.sum(jnp.where(causal, jnp.broadcast_to(lf_row, (l, l)), 0.0), axis=1, keepdims=True)
    b_row = _col_to_row(b_col, l)
    i_col = _row_to_col(i_row, l)

    m_prev = m_ref[0, 0, :, 0:1]
    dmat = jnp.where(causal, b_col - b_row + i_row, -jnp.inf)
    inter = b_col + m_prev
    m_t = jnp.maximum(inter, jnp.max(dmat, axis=1, keepdims=True))
    w_inter = jnp.exp(inter - m_t)

    qb = qf.astype(_BF)
    kb = kf.astype(_BF)
    s = lax.dot_general(qb, kb, _NT, preferred_element_type=_F32) * jnp.exp(dmat - m_t)
    c_old = c_ref[0, 0]
    n_old = n_ref[0, 0]
    num = (jnp.dot(s.astype(_BF), vb, preferred_element_type=_F32)
           + w_inter * jnp.dot(qb, c_old.astype(_BF), preferred_element_type=_F32))
    den = jnp.sum(s, axis=1, keepdims=True) + w_inter * jnp.sum(qf * n_old, axis=1, keepdims=True)
    hid = num / jnp.maximum(jnp.abs(den), jnp.exp(-m_t))
    tok = jax.nn.sigmoid(rows(og_ref)) * hid
    tok_ref[...] = tok[0:l_in]

    m_new = m_t[l - 1:l, :]
    b_last = b_col[l - 1:l, :]
    w_end = jnp.exp(b_last - b_col + i_col - m_new)
    decay = jnp.exp(b_last + m_prev - m_new)
    kw = kf * w_end
    c_ref[0, 0] = decay * c_old + lax.dot_general(kw.astype(_BF), vb, _TN, preferred_element_type=_F32)
    n_ref[0, 0] = decay * n_old + jnp.sum(kw, axis=0, keepdims=True)
    m_ref[0, 0] = jnp.broadcast_to(m_new, (1, 128))


def _mlstm(u, gates_t, b_if, c0, n0, m0, batch, seq):
    n = batch * seq
    l = min(seq, M_CHUNK_MAX)
    nc = seq // l
    l_pad = max(l, M_CHUNK_MIN)
    gi = gates_t[:M_HEADS].reshape(M_HEADS, n // l, 1, l)
    gf = gates_t[M_HEADS:].reshape(M_HEADS, n // l, 1, l)
    if l_pad != l:
        gi = jnp.pad(gi, ((0, 0), (0, 0), (0, 0), (0, l_pad - l)), constant_values=NEG_BIG)
        gf = jnp.pad(gf, ((0, 0), (0, 0), (0, 0), (0, l_pad - l)), constant_values=-NEG_BIG)
    vq = M_DV // M_DQK
    row = lambda b, h, c: b * nc + c
    state_c = pl.BlockSpec((1, 1, M_DQK, M_DV), lambda b, h, c: (b, h, 0, 0))
    state_n = pl.BlockSpec((1, 1, 1, M_DQK), lambda b, h, c: (b, h, 0, 0))
    state_m = pl.BlockSpec((1, 1, 1, 128), lambda b, h, c: (b, h, 0, 0))
    gate = pl.BlockSpec((1, 1, 1, l_pad), lambda b, h, c: (h, row(b, h, c), 0, 0))
    tok, c_new, n_new, m_new = pl.pallas_call(
        functools.partial(_mlstm_kernel, l_in=l, l=l_pad),
        grid=(batch, M_HEADS, nc),
        in_specs=[
            pl.BlockSpec(memory_space=pltpu.SMEM),
            pl.BlockSpec((l, M_DQK), lambda b, h, c: (row(b, h, c), h)),
            pl.BlockSpec((l, M_DQK), lambda b, h, c: (row(b, h, c), M_HEADS + h)),
            pl.BlockSpec((l, M_DV), lambda b, h, c: (row(b, h, c), M_HEADS + h)),
            pl.BlockSpec((l, M_DV), lambda b, h, c: (row(b, h, c), 2 * M_HEADS + h)),
            gate, gate, state_c, state_n, state_m,
        ],
        out_specs=[pl.BlockSpec((l, M_DV), lambda b, h, c: (row(b, h, c), h)), state_c, state_n, state_m],
        out_shape=[
            jax.ShapeDtypeStruct((n, MIX_TOKEN), _F32),
            jax.ShapeDtypeStruct((batch, M_HEADS, M_DQK, M_DV), _F32),
            jax.ShapeDtypeStruct((batch, M_HEADS, 1, M_DQK), _F32),
            jax.ShapeDtypeStruct((batch, M_HEADS, 1, 128), _F32),
        ],
        compiler_params=_params("parallel", "parallel", "arbitrary"),
    )(b_if, u, u, u, u, gi, gf, c0, n0.reshape(batch, M_HEADS, 1, M_DQK),
      jnp.broadcast_to(m0[:, :, None, None], (batch, M_HEADS, 1, 128)))
    assert vq * M_DQK == M_DV
    return tok, (c_new, n_new.reshape(batch, M_HEADS, M_DQK), m_new[:, :, 0, 0])


def _memattn_kernel(q_ref, k_ref, v_ref, o_ref):
    q = q_ref[...].astype(_BF)
    logits = lax.dot_general(q, k_ref[0].astype(_BF), _NT, preferred_element_type=_F32) * (MEM_DH ** -0.5)
    e = jnp.exp(logits - jnp.max(logits, axis=-1, keepdims=True))
    p = e / jnp.sum(e, axis=-1, keepdims=True)
    o_ref[...] = jnp.dot(p.astype(_BF), v_ref[0].astype(_BF), preferred_element_type=_F32)


def _mem_attend(qsrc, q_col0, mk, mv, k_col0, v_col0, batch, seq):
    n = batch * seq
    tq = min(seq, 512)
    nq = seq // tq
    qc, kc, vc = q_col0 // MEM_DH, k_col0 // MEM_DH, v_col0 // MEM_DH
    return pl.pallas_call(
        _memattn_kernel,
        grid=(batch, nq, MEM_HEADS),
        in_specs=[
            pl.BlockSpec((tq, MEM_DH), lambda b, t, h: (b * nq + t, qc + h)),
            pl.BlockSpec((1, MEM_SLOTS, MEM_DH), lambda b, t, h: (b, 0, kc + h)),
            pl.BlockSpec((1, MEM_SLOTS, MEM_DH), lambda b, t, h: (b, 0, vc + h)),
        ],
        out_specs=pl.BlockSpec((tq, MEM_DH), lambda b, t, h: (b * nq + t, h)),
        out_shape=jax.ShapeDtypeStruct((n, MIX_MEM), _F32),
        compiler_params=_params("parallel", "parallel", "parallel"),
    )(qsrc, mk, mv)


def _rel_bucket(dist):
    n = jnp.maximum(dist, 0)
    max_exact = REL_BUCKETS // 2
    nf = jnp.maximum(n, 1).astype(_F32)
    large = max_exact + (jnp.log(nf / max_exact) / math.log(REL_MAX_DIST / max_exact)
                         * (REL_BUCKETS - max_exact)).astype(jnp.int32)
    return jnp.where(n < max_exact, n, jnp.minimum(large, REL_BUCKETS - 1))


def _bias_lookup(rb_ref, head, dist):
    bucket = _rel_bucket(dist)
    val = jnp.full(dist.shape, rb_ref[0, head], _F32)
    for b in range(1, REL_BUCKETS):
        val = jnp.where(bucket == b, rb_ref[b, head], val)
    return val


def _bias_tiles_kernel(rb_ref, o_ref):
    head, delta = pl.program_id(0), pl.program_id(1)
    key = lax.broadcasted_iota(jnp.int32, (MOBA_BLOCK, MOBA_BLOCK), 0)
    qry = lax.broadcasted_iota(jnp.int32, (MOBA_BLOCK, MOBA_BLOCK), 1)
    o_ref[0, 0] = _bias_lookup(rb_ref, head, delta * MOBA_BLOCK + qry - key)


def _bias_tiles(rel_bias, nb):
    return pl.pallas_call(
        _bias_tiles_kernel,
        grid=(B_HEADS, nb),
        in_specs=[pl.BlockSpec(memory_space=pltpu.SMEM)],
        out_specs=pl.BlockSpec((1, 1, MOBA_BLOCK, MOBA_BLOCK), lambda h, d: (h, d, 0, 0)),
        out_shape=jax.ShapeDtypeStruct((B_HEADS, nb, MOBA_BLOCK, MOBA_BLOCK), _F32),
        compiler_params=_params("parallel", "parallel"),
    )(rel_bias)


def _moba_prompt_kernel(q_ref, k_ref, v_ref, bias_ref, o_ref, s_scr, *, nb):
    blk = MOBA_BLOCK
    scale = B_DH ** -0.5
    means = jnp.concatenate(
        [jnp.mean(k_ref[c * blk:(c + 1) * blk, :], axis=0, keepdims=True) for c in range(nb)], axis=0)
    gate = lax.dot_general(means, q_ref[...], _NT, preferred_element_type=_F32,
                           precision=lax.Precision.HIGHEST)
    key = lax.broadcasted_iota(jnp.int32, (blk, blk), 0)
    qry = lax.broadcasted_iota(jnp.int32, (blk, blk), 1)
    causal_pen = jnp.where(key <= qry, 0.0, -jnp.inf)
    kb = [k_ref[c * blk:(c + 1) * blk, :].astype(_BF) for c in range(nb)]
    vb = [v_ref[c * blk:(c + 1) * blk, :].astype(_BF) for c in range(nb)]

    slot = 0
    for j in range(nb):
        qb = q_ref[j * blk:(j + 1) * blk, :].astype(_BF)
        g = [gate[c:c + 1, j * blk:(j + 1) * blk] for c in range(j)]
        pens = []
        for c in range(j):
            if j <= MOBA_TOPK:
                pens.append(None)
                continue
            rank = jnp.zeros((1, blk), _F32)
            for c2 in range(j):
                if c2 != c:
                    rank = rank + jnp.where(g[c2] >= g[c] if c2 < c else g[c2] > g[c], 1.0, 0.0)
            pens.append(jnp.where(rank < MOBA_TOPK, 0.0, -jnp.inf))

        m_run = jnp.full((1, blk), -jnp.inf, _F32)
        for c in range(j + 1):
            st = lax.dot_general(kb[c], qb, _NT, preferred_element_type=_F32) * scale + bias_ref[0, j - c]
            if c == j:
                st = st + causal_pen
            elif pens[c] is not None:
                st = st + pens[c]
            s_scr[slot + c] = st
            m_run = jnp.maximum(m_run, jnp.max(st, axis=0, keepdims=True))
        l_run = jnp.zeros((1, blk), _F32)
        acc = jnp.zeros((blk, B_DH), _F32)
        for c in range(j + 1):
            p = jnp.exp(s_scr[slot + c] - m_run)
            l_run = l_run + jnp.sum(p, axis=0, keepdims=True)
            acc = acc + lax.dot_general(p.astype(_BF), vb[c], _TN, preferred_element_type=_F32)
        o_ref[j * blk:(j + 1) * blk, :] = acc / _row_to_col(l_run, blk)
        slot += j + 1


def _moba_prompt(qsrc, k, v, bias, batch, seq):
    n = batch * seq
    nb = seq // MOBA_BLOCK
    per_head = pl.BlockSpec((seq, B_DH), lambda h, b: (b, h))
    return pl.pallas_call(
        functools.partial(_moba_prompt_kernel, nb=nb),
        grid=(B_HEADS, batch),
        in_specs=[per_head, per_head, per_head,
                  pl.BlockSpec((1, nb, MOBA_BLOCK, MOBA_BLOCK), lambda h, b: (h, 0, 0, 0))],
        out_specs=per_head,
        out_shape=jax.ShapeDtypeStruct((n, MIX_TOKEN), _F32),
        scratch_shapes=[pltpu.VMEM((nb * (nb + 1) // 2, MOBA_BLOCK, MOBA_BLOCK), _F32)],
        compiler_params=_params("parallel", "parallel"),
    )(qsrc, k, v, bias)


def _page_means_kernel(pt_ref, ka_ref, kb_ref, o_ref):
    o_ref[0, 0] = (jnp.sum(ka_ref[0], axis=0) + jnp.sum(kb_ref[0], axis=0)) / MOBA_BLOCK


def _page_means(k_pool, page_table, nfull):
    batch = page_table.shape[0]
    assert MOBA_BLOCK == 2 * PAGE_SIZE
    page = lambda which: pl.BlockSpec((1, PAGE_SIZE, B_HEADS, B_DH), lambda b, c, pt: (pt[b, 2 * c + which], 0, 0, 0))
    grid_spec = pltpu.PrefetchScalarGridSpec(
        num_scalar_prefetch=1,
        grid=(batch, nfull),
        in_specs=[page(0), page(1)],
        out_specs=pl.BlockSpec((1, 1, B_HEADS, B_DH), lambda b, c, pt: (b, c, 0, 0)),
    )
    return pl.pallas_call(
        _page_means_kernel,
        grid_spec=grid_spec,
        out_shape=jax.ShapeDtypeStruct((batch, nfull, B_HEADS, B_DH), _F32),
        compiler_params=_params("parallel", "parallel"),
    )(page_table, k_pool, k_pool)


def _moba_sample_select_kernel(q_ref, means_ref, sel_ref, *, seq, nfull):
    lanes = 128
    col = lax.broadcasted_iota(jnp.int32, (seq, lanes), 1)
    colf = col.astype(_F32)
    out = jnp.zeros((seq, lanes), _F32)
    for h in range(B_HEADS):
        qh = q_ref[:, h * B_DH:(h + 1) * B_DH]
        mh = means_ref[0, :, h, :]
        mh = jnp.concatenate([mh, jnp.zeros((lanes - nfull, B_DH), _F32)], axis=0)
        gate = lax.dot_general(qh, mh, _NT, preferred_element_type=_F32, precision=lax.Precision.HIGHEST)
        gate = jnp.where(col < nfull, gate, -jnp.inf)
        for kk in range(MOBA_TOPK):
            best = jnp.max(gate, axis=1, keepdims=True)
            idx = jnp.min(jnp.where(gate == best, colf, float(lanes)), axis=1, keepdims=True)
            out = jnp.where(col == h * MOBA_TOPK + kk, idx, out)
            gate = jnp.where(colf == idx, -jnp.inf, gate)
    sel_ref[0] = out.astype(jnp.int32)


def _moba_sample_select(qsrc, means, batch, seq):
    nfull = means.shape[1]
    assert nfull >= MOBA_TOPK and nfull <= 128 and B_HEADS * MOBA_TOPK <= 128
    return pl.pallas_call(
        functools.partial(_moba_sample_select_kernel, seq=seq, nfull=nfull),
        grid=(batch,),
        in_specs=[
            pl.BlockSpec((seq, MIX_TOKEN), lambda b: (b, 0)),
            pl.BlockSpec((1, nfull, B_HEADS, B_DH), lambda b: (b, 0, 0, 0)),
        ],
        out_specs=pl.BlockSpec((1, seq, 128), lambda b: (b, 0, 0)),
        out_shape=jax.ShapeDtypeStruct((batch, seq, 128), jnp.int32),
        compiler_params=_params("parallel"),
    )(qsrc, means)


def _moba_sample_kernel(sel_ref, pt_ref, q_ref, kn_ref, vn_ref, rb_ref, kpool_ref, vpool_ref, o_ref,
                        kbuf, vbuf, sems, *, seq, past):
    b, h = pl.program_id(0), pl.program_id(1)
    blk = MOBA_BLOCK
    ppb = blk // PAGE_SIZE
    nsel = seq * MOBA_TOPK

    def copies(t, kk, pg):
        block = sel_ref[b, t, h * MOBA_TOPK + kk]
        page = pt_ref[b, block * ppb + pg]
        dst = pl.ds(((t * MOBA_TOPK + kk) * ppb + pg) * PAGE_SIZE, PAGE_SIZE)
        return (pltpu.make_async_copy(kpool_ref.at[page, :, h, :], kbuf.at[dst, :], sems.at[0]),
                pltpu.make_async_copy(vpool_ref.at[page, :, h, :], vbuf.at[dst, :], sems.at[1]))

    every = [(t, kk, pg) for t in range(seq) for kk in range(MOBA_TOPK) for pg in range(ppb)]
    for idx in every:
        for cp in copies(*idx):
            cp.start()

    qf = q_ref[...]
    qb = qf.astype(_BF)
    scale = B_DH ** -0.5
    pad = jnp.zeros((128 - seq, B_DH), _F32)
    k_own = jnp.concatenate([kn_ref[...], pad], axis=0).astype(_BF)
    v_own = jnp.concatenate([vn_ref[...], pad], axis=0).astype(_BF)
    t_own = lax.broadcasted_iota(jnp.int32, (seq, 128), 0)
    s_own = lax.broadcasted_iota(jnp.int32, (seq, 128), 1)
    lo = (lax.dot_general(qb, k_own, _NT, preferred_element_type=_F32) * scale
          + _bias_lookup(rb_ref, h, t_own - s_own))
    lo = jnp.where(s_own <= t_own, lo, -jnp.inf)

    ncol = nsel * blk
    t_sel = lax.broadcasted_iota(jnp.int32, (seq, ncol), 0)
    col = lax.broadcasted_iota(jnp.int32, (seq, ncol), 1)
    slot = col >> int(math.log2(blk))
    block_of_col = jnp.zeros((seq, ncol), jnp.int32)
    token_of_col = jnp.zeros((seq, ncol), jnp.int32)
    for t in range(seq):
        for kk in range(MOBA_TOPK):
            here = slot == t * MOBA_TOPK + kk
            block_of_col = jnp.where(here, sel_ref[b, t, h * MOBA_TOPK + kk], block_of_col)
            token_of_col = jnp.where(here, t, token_of_col)
    dist = past + t_sel - (block_of_col * blk + (col & (blk - 1)))
    bias_sel = _bias_lookup(rb_ref, h, dist)

    for idx in every:
        for cp in copies(*idx):
            cp.wait()

    ls = lax.dot_general(qb, kbuf[...].astype(_BF), _NT, preferred_element_type=_F32) * scale + bias_sel
    ls = jnp.where(token_of_col == t_sel, ls, -jnp.inf)
    m = jnp.maximum(jnp.max(lo, axis=1, keepdims=True), jnp.max(ls, axis=1, keepdims=True))
    p_own = jnp.exp(lo - m)
    p_sel = jnp.exp(ls - m)
    denom = jnp.sum(p_own, axis=1, keepdims=True) + jnp.sum(p_sel, axis=1, keepdims=True)
    out = (jnp.dot(p_own.astype(_BF), v_own, preferred_element_type=_F32)
           + jnp.dot(p_sel.astype(_BF), vbuf[...].astype(_BF), preferred_element_type=_F32))
    o_ref[...] = out / denom


def _moba_sample(qsrc, k_new, v_new, sel, page_table, rel_bias, k_pool, v_pool, batch, seq):
    n = batch * seq
    past = page_table.shape[1] * PAGE_SIZE
    assert past % MOBA_BLOCK == 0 and (past + seq - 1) // MOBA_BLOCK == past // MOBA_BLOCK
    rows = seq * MOBA_TOPK * MOBA_BLOCK
    grid_spec = pltpu.PrefetchScalarGridSpec(
        num_scalar_prefetch=2,
        grid=(batch, B_HEADS),
        in_specs=[
            pl.BlockSpec((seq, B_DH), lambda b, h, *_: (b, h)),
            pl.BlockSpec((seq, B_DH), lambda b, h, *_: (b, h)),
            pl.BlockSpec((seq, B_DH), lambda b, h, *_: (b, h)),
            pl.BlockSpec(memory_space=pltpu.SMEM),
            pl.BlockSpec(memory_space=pl.ANY),
            pl.BlockSpec(memory_space=pl.ANY),
        ],
        out_specs=pl.BlockSpec((seq, B_DH), lambda b, h, *_: (b, h)),
        scratch_shapes=[pltpu.VMEM((rows, B_DH), _F32), pltpu.VMEM((rows, B_DH), _F32),
                        pltpu.SemaphoreType.DMA((2,))],
    )
    return pl.pallas_call(
        functools.partial(_moba_sample_kernel, seq=seq, past=past),
        grid_spec=grid_spec,
        out_shape=jax.ShapeDtypeStruct((n, MIX_TOKEN), _F32),
        compiler_params=_params("arbitrary", "arbitrary"),
    )(sel, page_table, qsrc, k_new, v_new, rel_bias, k_pool, v_pool)


def _top_sorted(scores, count):
    tm = scores.shape[1]
    rank = lax.broadcasted_iota(jnp.int32, (count, tm), 0)
    vals = []
    stacked = jnp.zeros((count, tm), _F32)
    cur = scores
    for r in range(count):
        best = jnp.max(cur, axis=0, keepdims=True)
        vals.append(best)
        stacked = jnp.where(rank == r, best, stacked)
        cur = jnp.where(cur >= best, -jnp.inf, cur)
    return vals, stacked


def _peer_candidates(a, a_st, b, b_st):
    tm = a_st.shape[1]
    half = PEER_TOPK // 2
    groups = [a[0] + b_st[0:half], a[0] + b_st[half:PEER_TOPK]]
    groups += [a[i] + b_st[0:half] for i in range(1, half)]
    groups.append(a_st[half:PEER_TOPK] + b[0])
    return jnp.concatenate([jnp.broadcast_to(g, (half, tm)) for g in groups], axis=0)


def _peer_select_kernel(q_ref, keys_ref, s1_ref, s2_ref, e1_ref, e2_ref, tau_ref):
    tm = q_ref.shape[0]
    for h in range(PEER_HEADS):
        qh = q_ref[:, h * PEER_DKEY:(h + 1) * PEER_DKEY]
        st = lax.dot_general(keys_ref[h], qh, _NT, preferred_element_type=_F32,
                             precision=lax.Precision.HIGHEST)
        s1 = st[0:PEER_NKEYS]
        s2 = st[PEER_NKEYS:2 * PEER_NKEYS]
        a, a_st = _top_sorted(s1, PEER_TOPK)
        bb, b_st = _top_sorted(s2, PEER_TOPK)
        cand = _peer_candidates(a, a_st, bb, b_st)
        tau = jnp.zeros((1, tm), _F32)
        taken = jnp.zeros((1, tm), _F32)
        cur = cand
        for _ in range(PEER_TOPK):
            best = jnp.max(cur, axis=0, keepdims=True)
            hit = cur == best
            tau = jnp.where(taken < PEER_TOPK, best, tau)
            taken = taken + jnp.sum(jnp.where(hit, 1.0, 0.0), axis=0, keepdims=True)
            cur = jnp.where(hit, -jnp.inf, cur)
        top = a[0] + bb[0]
        z = jnp.sum(jnp.where(cand >= tau, jnp.exp(cand - top), 0.0), axis=0, keepdims=True)
        s1_ref[h] = s1
        s2_ref[h] = s2
        e1_ref[h] = jnp.exp(s1 - a[0]) / z
        e2_ref[h] = jnp.exp(s2 - bb[0])
        tau_ref[h] = jnp.broadcast_to(tau, (8, tm))


def _peer_select(q, keys_bd):
    n = q.shape[0]
    tm = min(256, n)
    half = pl.BlockSpec((PEER_HEADS, PEER_NKEYS, tm), lambda i: (0, 0, i))
    half_shape = jax.ShapeDtypeStruct((PEER_HEADS, PEER_NKEYS, n), _F32)
    return pl.pallas_call(
        _peer_select_kernel,
        grid=(n // tm,),
        in_specs=[pl.BlockSpec((tm, PEER_HEADS * PEER_DKEY), lambda i: (i, 0)),
                  pl.BlockSpec((PEER_HEADS, 2 * PEER_NKEYS, PEER_DKEY), lambda i: (0, 0, 0))],
        out_specs=[half, half, half, half, pl.BlockSpec((PEER_HEADS, 8, tm), lambda i: (0, 0, i))],
        out_shape=[half_shape, half_shape, half_shape, half_shape,
                   jax.ShapeDtypeStruct((PEER_HEADS, 8, n), _F32)],
        compiler_params=_params("parallel"),
    )(q, keys_bd)


PEER_TE = 1024
PEER_TM = 512
PEER_TM_GROUP = 512
PEER_SPLIT = 4
PEER_D_CHUNK = 2048


def _peer_dense_kernel(x_ref, u_ref, v_ref, s1_ref, s2_ref, e1_ref, e2_ref, tau_ref, o_ref, *, te, group):
    @pl.when(pl.program_id(1) == 0)
    def _():
        o_ref[...] = jnp.zeros_like(o_ref)

    tm = x_ref.shape[0]
    sub = te // PEER_SPLIT
    rows = sub // PEER_NKEYS
    for g0 in range(0, tm, group):
        cols = slice(g0, g0 + group)
        xg = x_ref[cols, :]
        a_t = [lax.dot_general(u_ref[s * sub:(s + 1) * sub, :], xg, _NT, preferred_element_type=_F32)
               for s in range(PEER_SPLIT)]
        for s in range(PEER_SPLIT):
            parts = []
            for r in range(rows):
                k1 = s * rows + r
                w = jnp.zeros((PEER_NKEYS, group), _F32)
                for h in range(PEER_HEADS):
                    total = s1_ref[h, k1:k1 + 1, cols] + s2_ref[h, :, cols]
                    w = w + jnp.where(total >= tau_ref[h, 0:1, cols],
                                      e2_ref[h, :, cols] * e1_ref[h, k1:k1 + 1, cols], 0.0)
                act = jax.nn.gelu(a_t[s][r * PEER_NKEYS:(r + 1) * PEER_NKEYS])
                parts.append((act * w).astype(_BF))
            gated = jnp.concatenate(parts, axis=0)
            for d0 in range(0, o_ref.shape[0], PEER_D_CHUNK):
                o_ref[d0:d0 + PEER_D_CHUNK, cols] += lax.dot_general(
                    v_ref[s * sub:(s + 1) * sub, d0:d0 + PEER_D_CHUNK], gated, _TN, preferred_element_type=_F32)


def _peer_dense(xb, u_stack, v_stack, layer, sel):
    n, d = xb.shape
    tm = min(PEER_TM, n)
    te = PEER_TE
    rows = te // PEER_NKEYS
    assert rows == 8
    held = pl.Buffered(1)
    per_key2 = pl.BlockSpec((PEER_HEADS, PEER_NKEYS, tm), lambda i, e: (0, 0, i), pipeline_mode=held)
    per_key1 = pl.BlockSpec((PEER_HEADS, rows, tm), lambda i, e: (0, e, i))
    s1, s2, e1, e2, tau = sel
    return pl.pallas_call(
        functools.partial(_peer_dense_kernel, te=te, group=min(PEER_TM_GROUP, tm)),
        grid=(n // tm, PEER_N // te),
        in_specs=[pl.BlockSpec((tm, d), lambda i, e: (i, 0), pipeline_mode=held),
                  pl.BlockSpec((None, te, d), lambda i, e: (layer, e, 0)),
                  pl.BlockSpec((None, te, d), lambda i, e: (layer, e, 0)),
                  per_key1, per_key2, per_key1, per_key2,
                  pl.BlockSpec((PEER_HEADS, 8, tm), lambda i, e: (0, 0, i), pipeline_mode=held)],
        out_specs=pl.BlockSpec((d, tm), lambda i, e: (0, i), pipeline_mode=held),
        out_shape=jax.ShapeDtypeStruct((d, n), _F32),
        compiler_params=_params("parallel", "arbitrary"),
    )(xb, u_stack, v_stack, s1, s2, e1, e2, tau)


def _peer(xb, layer, wq_stack, keys_bd, u_stack, v_stack):
    n = xb.shape[0]
    n_pad = max(n, 128)
    if n_pad != n:
        xb = jnp.pad(xb, ((0, n_pad - n), (0, 0)))
    q = _matmul([xb], [(wq_stack, layer, 0)], 1024, 512)
    return _peer_dense(xb, u_stack, v_stack, layer, _peer_select(q, keys_bd))


def _layer_tail(x, tok, mem, layer, w):
    n = x.shape[0]
    mixed = _matmul([tok, mem], [(w['out'], layer, 0), (w['out'], layer, MIX_TOKEN)], 512, 512)
    x1, x1b = _add_layernorm(x, mixed, *w['ln1'][layer])
    ffn_t = _peer(x1b, layer, w['peer_wq'], w['peer_keys'][layer], w['peer_u'], w['peer_v'])
    if ffn_t.shape[1] == n:
        return _add_layernorm(x1, ffn_t, *w['ln2'][layer], y_transposed=True)
    return _add_layernorm(x1, ffn_t[:, :n].T, *w['ln2'][layer])


def _trunk(x, batch, seq, mem_kv, mlstm_init, moba_attend, w):
    xb = x.astype(_BF)
    u = _matmul([xb], [w['in_a_main']], 1024, 512)
    qm = _matmul([xb], [w['in_a_mem']], 1024, 512)
    gates = _matmul([x], [w['in_a_gates']], 512, 128, precision=lax.Precision.HIGHEST)
    tok, state = _mlstm(u, gates[:, :2 * M_HEADS].T, w['b_if'], *mlstm_init, batch, seq)
    mem = _mem_attend(qm, 0, *mem_kv[0], batch, seq)
    x, xb = _layer_tail(x, tok, mem, 0, w)
    k_sh = _matmul([xb], [w['k_shared']], 1024, 512)
    v_sh = _matmul([xb], [w['v_shared']], 1024, 512)
    ub = _matmul([xb], [w['in_b']], 1024, 512)
    tok = moba_attend(ub, k_sh, v_sh)
    mem = _mem_attend(ub, MIX_TOKEN, *mem_kv[1], batch, seq)
    x, _ = _layer_tail(x, tok, mem, 1, w)
    return x, state, k_sh, v_sh


def kernel(x_prompt, x_sample, cache_moba_k, cache_moba_v, cache_mem_k, cache_mem_v, state_mlstm_c,
           state_mlstm_n, state_mlstm_m, page_table, mem_prompt, w_in_a, b_if_a, w_in_b, w_kv_shared, rel_bias,
           w_mem_kv, w_out, ln1_g, ln1_b, ln2_g, ln2_b, peer_wq, peer_keys, peer_u, peer_v):
    bp, tp, d = x_prompt.shape
    bs, ts, _ = x_sample.shape
    gate0 = 2 * M_HEADS * M_DQK + 2 * MIX_TOKEN
    wa = w_in_a[0]
    zeros = jnp.zeros((PEER_HEADS, PEER_NKEYS, PEER_DKEY // 2), _F32)

    def keys_blockdiag(keys):
        return jnp.concatenate([jnp.concatenate([keys[:, 0], zeros], axis=2),
                                jnp.concatenate([zeros, keys[:, 1]], axis=2)], axis=1)

    w = {
        'in_a_main': wa[:, :gate0].astype(_BF),
        'in_a_mem': wa[:, gate0 + 2 * M_HEADS:].astype(_BF),
        'in_a_gates': jnp.pad(wa[:, gate0:gate0 + 2 * M_HEADS], ((0, 0), (0, 128 - 2 * M_HEADS))),
        'b_if': b_if_a[0],
        'in_b': w_in_b[0].astype(_BF),
        'k_shared': w_kv_shared[:, :MIX_TOKEN].astype(_BF),
        'v_shared': w_kv_shared[:, MIX_TOKEN:].astype(_BF),
        'out': w_out.astype(_BF),
        'ln1': [(ln1_g[l], ln1_b[l]) for l in range(DEPTH)],
        'ln2': [(ln2_g[l], ln2_b[l]) for l in range(DEPTH)],
        'peer_wq': peer_wq.astype(_BF),
        'peer_keys': [keys_blockdiag(peer_keys[l]) for l in range(DEPTH)],
        'peer_u': peer_u.astype(_BF),
        'peer_v': peer_v.astype(_BF),
    }

    mem_w = jnp.concatenate([w_mem_kv[l] for l in range(DEPTH)], axis=1).astype(_BF)
    mkv = _matmul([mem_prompt.reshape(bp * MEM_SLOTS, d).astype(_BF)], [mem_w], 1024, 512)
    mkv3 = mkv.reshape(bp, MEM_SLOTS, 2 * DEPTH * MIX_MEM)
    mem_kv_p = [(mkv3, mkv3, 2 * l * MIX_MEM, (2 * l + 1) * MIX_MEM) for l in range(DEPTH)]
    init_p = (jnp.zeros((bp, M_HEADS, M_DQK, M_DV), _F32), jnp.zeros((bp, M_HEADS, M_DQK), _F32),
              jnp.zeros((bp, M_HEADS), _F32))
    bias = _bias_tiles(rel_bias, tp // MOBA_BLOCK)
    y_p, st_p, k_p, v_p = _trunk(
        x_prompt.reshape(bp * tp, d), bp, tp, mem_kv_p, init_p,
        lambda ub, k_sh, v_sh: _moba_prompt(ub, k_sh, v_sh, bias, bp, tp), w)

    mem_kv_s = [(cache_mem_k[l].reshape(bs, MEM_SLOTS, MIX_MEM), cache_mem_v[l].reshape(bs, MEM_SLOTS, MIX_MEM), 0, 0)
                for l in range(DEPTH)]
    init_s = (state_mlstm_c[0], state_mlstm_n[0], state_mlstm_m[0])
    nfull = page_table.shape[1] * PAGE_SIZE // MOBA_BLOCK

    def moba_sample(ub, k_sh, v_sh):
        means = _page_means(cache_moba_k, page_table, nfull)
        sel = _moba_sample_select(ub, means, bs, ts)
        return _moba_sample(ub, k_sh, v_sh, sel, page_table, rel_bias, cache_moba_k, cache_moba_v, bs, ts)

    y_s, st_s, k_s, v_s = _trunk(x_sample.reshape(bs * ts, d), bs, ts, mem_kv_s, init_s, moba_sample, w)

    mem_k_p = jnp.stack([mkv3[:, :, 2 * l * MIX_MEM:(2 * l + 1) * MIX_MEM] for l in range(DEPTH)])
    mem_v_p = jnp.stack([mkv3[:, :, (2 * l + 1) * MIX_MEM:(2 * l + 2) * MIX_MEM] for l in range(DEPTH)])
    kv_shape = (bp, MEM_SLOTS, MEM_HEADS, MEM_DH)
    return (y_p.reshape(bp, tp, d), y_s.reshape(bs, ts, d),
            st_p[0][None], st_p[1][None], st_p[2][None],
            k_p.reshape(bp, tp, B_HEADS, B_DH), v_p.reshape(bp, tp, B_HEADS, B_DH),
            mem_k_p.reshape((DEPTH,) + kv_shape), mem_v_p.reshape((DEPTH,) + kv_shape),
            st_s[0][None], st_s[1][None], st_s[2][None],
            k_s.reshape(bs, ts, B_HEADS, B_DH), v_s.reshape(bs, ts, B_HEADS, B_DH))
```

```python
import functools
import math

import jax
import jax.numpy as jnp
from jax import lax
from jax.experimental import pallas as pl
from jax.experimental.pallas import tpu as pltpu

D_MODEL = 4096
DEPTH = 2
PAGE_SIZE = 128
MIX_TOKEN = 3 * D_MODEL // 4
MIX_MEM = D_MODEL // 4
M_HEADS = 6
M_DV = MIX_TOKEN // M_HEADS
M_DQK = M_DV // 2
M_CHUNK_MAX = 256
M_CHUNK_MIN = 128
B_HEADS = 24
B_DH = MIX_TOKEN // B_HEADS
MOBA_BLOCK = 256
MOBA_TOPK = 3
MEM_SLOTS = 256
MEM_HEADS = 4
MEM_DH = MIX_MEM // MEM_HEADS
REL_BUCKETS = 32
REL_MAX_DIST = 4096
PEER_HEADS = 8
PEER_NKEYS = 128
PEER_N = PEER_NKEYS * PEER_NKEYS
PEER_DKEY = 128
PEER_TOPK = 16
DN_ALPHA = (2.0 * DEPTH) ** 0.25
LN_EPS = 1e-5

VMEM_LIMIT_V7X = 56 * 1024 * 1024
NEG_BIG = -1e30

_NT = (((1,), (1,)), ((), ()))
_TN = (((0,), (0,)), ((), ()))
_BF = jnp.bfloat16
_F32 = jnp.float32


def _params(*sem):
    return pltpu.CompilerParams(dimension_semantics=sem, vmem_limit_bytes=VMEM_LIMIT_V7X)


def _row_to_col(row, n):
    eye = lax.broadcasted_iota(jnp.int32, (n, n), 0) == lax.broadcasted_iota(jnp.int32, (n, n), 1)
    return jnp.sum(jnp.where(eye, jnp.broadcast_to(row, (n, n)), 0.0), axis=1, keepdims=True)


def _col_to_row(col, n):
    eye = lax.broadcasted_iota(jnp.int32, (n, n), 0) == lax.broadcasted_iota(jnp.int32, (n, n), 1)
    return jnp.sum(jnp.where(eye, jnp.broadcast_to(col, (n, n)), 0.0), axis=0, keepdims=True)


def _mm_kernel(*refs, n_pairs, precision):
    o_ref = refs[2 * n_pairs]
    acc = None
    for a_ref, w_ref in zip(refs[:n_pairs], refs[n_pairs:2 * n_pairs]):
        a = a_ref[...]
        if precision is None:
            a = a.astype(_BF)
        d = jnp.dot(a, w_ref[...], preferred_element_type=_F32, precision=precision)
        acc = d if acc is None else acc + d
    o_ref[...] = acc


def _weight_spec(w, k, tn):
    if not isinstance(w, tuple):
        return w, pl.BlockSpec((k, tn), lambda i, j: (0, j))
    stack, layer, row0 = w
    assert row0 % k == 0
    return stack, pl.BlockSpec((None, k, tn), lambda i, j: (layer, row0 // k, j))


def _matmul(a_list, w_list, tm, tn, precision=None):
    n = a_list[0].shape[0]
    w0 = w_list[0]
    n_out = (w0[0] if isinstance(w0, tuple) else w0).shape[-1]
    tm = min(tm, n)
    tn = min(tn, n_out)
    views = [_weight_spec(w, a.shape[1], tn) for a, w in zip(a_list, w_list)]
    in_specs = [pl.BlockSpec((tm, a.shape[1]), lambda i, j: (i, 0)) for a in a_list]
    in_specs += [spec for _, spec in views]
    return pl.pallas_call(
        functools.partial(_mm_kernel, n_pairs=len(a_list), precision=precision),
        grid=(n // tm, n_out // tn),
        in_specs=in_specs,
        out_specs=pl.BlockSpec((tm, tn), lambda i, j: (i, j)),
        out_shape=jax.ShapeDtypeStruct((n, n_out), _F32),
        compiler_params=_params("parallel", "parallel"),
    )(*a_list, *[arr for arr, _ in views])


def _mm_nt_kernel(a_ref, wt_ref, o_ref, *, precision):
    a = a_ref[...] if precision is not None else a_ref[...].astype(_BF)
    o_ref[...] = lax.dot_general(a, wt_ref[...], _NT, preferred_element_type=_F32, precision=precision)


def _matmul_nt(a, w_t, tm, tn, precision=None):
    n, k = a.shape
    n_out = w_t.shape[0]
    tm = min(tm, n)
    tn = min(tn, n_out)
    return pl.pallas_call(
        functools.partial(_mm_nt_kernel, precision=precision),
        grid=(n // tm, n_out // tn),
        in_specs=[pl.BlockSpec((tm, k), lambda i, j: (i, 0)), pl.BlockSpec((tn, k), lambda i, j: (j, 0))],
        out_specs=pl.BlockSpec((tm, tn), lambda i, j: (i, j)),
        out_shape=jax.ShapeDtypeStruct((n, n_out), _F32),
        compiler_params=_params("parallel", "parallel"),
    )(a, w_t)


def _ln_kernel(x_ref, y_ref, g_ref, b_ref, o_ref, ob_ref, *, y_transposed):
    y = y_ref[...].T if y_transposed else y_ref[...]
    z = DN_ALPHA * x_ref[...] + y
    mu = jnp.mean(z, axis=-1, keepdims=True)
    zc = z - mu
    var = jnp.mean(zc * zc, axis=-1, keepdims=True)
    y = zc * lax.rsqrt(var + LN_EPS) * g_ref[...] + b_ref[...]
    o_ref[...] = y
    ob_ref[...] = y.astype(_BF)


def _add_layernorm(x, y, g, b, y_transposed=False):
    n, d = x.shape
    tm = min(128, n)
    row = pl.BlockSpec((tm, d), lambda i: (i, 0))
    vec = pl.BlockSpec((1, d), lambda i: (0, 0))
    y_spec = pl.BlockSpec((d, tm), lambda i: (0, i)) if y_transposed else row
    return pl.pallas_call(
        functools.partial(_ln_kernel, y_transposed=y_transposed),
        grid=(n // tm,),
        in_specs=[row, y_spec, vec, vec],
        out_specs=[row, row],
        out_shape=[jax.ShapeDtypeStruct((n, d), _F32), jax.ShapeDtypeStruct((n, d), _BF)],
        compiler_params=_params("parallel"),
    )(x, y, g.reshape(1, d), b.reshape(1, d))


def _log_sigmoid(x):
    return jnp.minimum(x, 0.0) - jnp.log1p(jnp.exp(-jnp.abs(x)))


def _mlstm_kernel(bif_ref, q_ref, k_ref, v_ref, og_ref, gi_ref, gf_ref, c0_ref, n0_ref, m0_ref,
                  tok_ref, c_ref, n_ref, m_ref, *, l_in, l):
    h = pl.program_id(1)

    @pl.when(pl.program_id(2) == 0)
    def _():
        c_ref[...] = c0_ref[...]
        n_ref[...] = n0_ref[...]
        m_ref[...] = m0_ref[...]

    def rows(ref):
        x = ref[...]
        if l_in == l:
            return x
        return jnp.concatenate([x, jnp.zeros((l - l_in, x.shape[1]), x.dtype)], axis=0)

    qf = rows(q_ref)
    kf = rows(k_ref) * (M_DQK ** -0.5)
    vb = rows(v_ref).astype(_BF)
    i_row = gi_ref[0, 0] + bif_ref[h]
    f_row = gf_ref[0, 0] + bif_ref[M_HEADS + h]
    lf_row = _log_sigmoid(f_row)

    t_idx = lax.broadcasted_iota(jnp.int32, (l, l), 0)
    s_idx = lax.broadcasted_iota(jnp.int32, (l, l), 1)
    causal = s_idx <= t_idx
    b_col = jnp.sum(jnp.where(causal, jnp.broadcast_to(lf_row, (l, l)), 0.0), axis=1, keepdims=True)
    b_row = _col_to_row(b_col, l)
    i_col = _row_to_col(i_row, l)

    m_prev = m_ref[0, 0, :, 0:1]
    dmat = jnp.where(causal, b_col - b_row + i_row, -jnp.inf)
    inter = b_col + m_prev
    m_t = jnp.maximum(inter, jnp.max(dmat, axis=1, keepdims=True))
    w_inter = jnp.exp(inter - m_t)

    qb = qf.astype(_BF)
    kb = kf.astype(_BF)
    s = lax.dot_general(qb, kb, _NT, preferred_element_type=_F32) * jnp.exp(dmat - m_t)
    c_old = c_ref[0, 0]
    n_old = n_ref[0, 0]
    num = (jnp.dot(s.astype(_BF), vb, preferred_element_type=_F32)
           + w_inter * jnp.dot(qb, c_old.astype(_BF), preferred_element_type=_F32))
    den = jnp.sum(s, axis=1, keepdims=True) + w_inter * jnp.sum(qf * n_old, axis=1, keepdims=True)
    hid = num / jnp.maximum(jnp.abs(den), jnp.exp(-m_t))
    tok = jax.nn.sigmoid(rows(og_ref)) * hid
    tok_ref[...] = tok[0:l_in]

    m_new = m_t[l - 1:l, :]
    b_last = b_col[l - 1:l, :]
    w_end = jnp.exp(b_last - b_col + i_col - m_new)
    decay = jnp.exp(b_last + m_prev - m_new)
    kw = kf * w_end
    c_ref[0, 0] = decay * c_old + lax.dot_general(kw.astype(_BF), vb, _TN, preferred_element_type=_F32)
    n_ref[0, 0] = decay * n_old + jnp.sum(kw, axis=0, keepdims=True)
    m_ref[0, 0] = jnp.broadcast_to(m_new, (1, 128))


def _mlstm(u, gates_t, b_if, c0, n0, m0, batch, seq):
    n = batch * seq
    l = min(seq, M_CHUNK_MAX)
    nc = seq // l
    l_pad = max(l, M_CHUNK_MIN)
    gi = gates_t[:M_HEADS].reshape(M_HEADS, n // l, 1, l)
    gf = gates_t[M_HEADS:].reshape(M_HEADS, n // l, 1, l)
    if l_pad != l:
        gi = jnp.pad(gi, ((0, 0), (0, 0), (0, 0), (0, l_pad - l)), constant_values=NEG_BIG)
        gf = jnp.pad(gf, ((0, 0), (0, 0), (0, 0), (0, l_pad - l)), constant_values=-NEG_BIG)
    vq = M_DV // M_DQK
    row = lambda b, h, c: b * nc + c
    state_c = pl.BlockSpec((1, 1, M_DQK, M_DV), lambda b, h, c: (b, h, 0, 0))
    state_n = pl.BlockSpec((1, 1, 1, M_DQK), lambda b, h, c: (b, h, 0, 0))
    state_m = pl.BlockSpec((1, 1, 1, 128), lambda b, h, c: (b, h, 0, 0))
    gate = pl.BlockSpec((1, 1, 1, l_pad), lambda b, h, c: (h, row(b, h, c), 0, 0))
    tok, c_new, n_new, m_new = pl.pallas_call(
        functools.partial(_mlstm_kernel, l_in=l, l=l_pad),
        grid=(batch, M_HEADS, nc),
        in_specs=[
            pl.BlockSpec(memory_space=pltpu.SMEM),
            pl.BlockSpec((l, M_DQK), lambda b, h, c: (row(b, h, c), h)),
            pl.BlockSpec((l, M_DQK), lambda b, h, c: (row(b, h, c), M_HEADS + h)),
            pl.BlockSpec((l, M_DV), lambda b, h, c: (row(b, h, c), M_HEADS + h)),
            pl.BlockSpec((l, M_DV), lambda b, h, c: (row(b, h, c), 2 * M_HEADS + h)),
            gate, gate, state_c, state_n, state_m,
        ],
        out_specs=[pl.BlockSpec((l, M_DV), lambda b, h, c: (row(b, h, c), h)), state_c, state_n, state_m],
        out_shape=[
            jax.ShapeDtypeStruct((n, MIX_TOKEN), _F32),
            jax.ShapeDtypeStruct((batch, M_HEADS, M_DQK, M_DV), _F32),
            jax.ShapeDtypeStruct((batch, M_HEADS, 1, M_DQK), _F32),
            jax.ShapeDtypeStruct((batch, M_HEADS, 1, 128), _F32),
        ],
        compiler_params=_params("parallel", "parallel", "arbitrary"),
    )(b_if, u, u, u, u, gi, gf, c0, n0.reshape(batch, M_HEADS, 1, M_DQK),
      jnp.broadcast_to(m0[:, :, None, None], (batch, M_HEADS, 1, 128)))
    assert vq * M_DQK == M_DV
    return tok, (c_new, n_new.reshape(batch, M_HEADS, M_DQK), m_new[:, :, 0, 0])


def _memattn_kernel(q_ref, k_ref, v_ref, o_ref):
    q = q_ref[...].astype(_BF)
    logits = lax.dot_general(q, k_ref[0].astype(_BF), _NT, preferred_element_type=_F32) * (MEM_DH ** -0.5)
    e = jnp.exp(logits - jnp.max(logits, axis=-1, keepdims=True))
    p = e / jnp.sum(e, axis=-1, keepdims=True)
    o_ref[...] = jnp.dot(p.astype(_BF), v_ref[0].astype(_BF), preferred_element_type=_F32)


def _mem_attend(qsrc, q_col0, mk, mv, k_col0, v_col0, batch, seq):
    n = batch * seq
    tq = min(seq, 512)
    nq = seq // tq
    qc, kc, vc = q_col0 // MEM_DH, k_col0 // MEM_DH, v_col0 // MEM_DH
    return pl.pallas_call(
        _memattn_kernel,
        grid=(batch, nq, MEM_HEADS),
        in_specs=[
            pl.BlockSpec((tq, MEM_DH), lambda b, t, h: (b * nq + t, qc + h)),
            pl.BlockSpec((1, MEM_SLOTS, MEM_DH), lambda b, t, h: (b, 0, kc + h)),
            pl.BlockSpec((1, MEM_SLOTS, MEM_DH), lambda b, t, h: (b, 0, vc + h)),
        ],
        out_specs=pl.BlockSpec((tq, MEM_DH), lambda b, t, h: (b * nq + t, h)),
        out_shape=jax.ShapeDtypeStruct((n, MIX_MEM), _F32),
        compiler_params=_params("parallel", "parallel", "parallel"),
    )(qsrc, mk, mv)


def _rel_bucket(dist):
    n = jnp.maximum(dist, 0)
    max_exact = REL_BUCKETS // 2
    nf = jnp.maximum(n, 1).astype(_F32)
    large = max_exact + (jnp.log(nf / max_exact) / math.log(REL_MAX_DIST / max_exact)
                         * (REL_BUCKETS - max_exact)).astype(jnp.int32)
    return jnp.where(n < max_exact, n, jnp.minimum(large, REL_BUCKETS - 1))


def _bias_lookup(rb_ref, head, dist):
    bucket = _rel_bucket(dist)
    val = jnp.full(dist.shape, rb_ref[0, head], _F32)
    for b in range(1, REL_BUCKETS):
        val = jnp.where(bucket == b, rb_ref[b, head], val)
    return val


def _bias_tiles_kernel(rb_ref, o_ref):
    head, delta = pl.program_id(0), pl.program_id(1)
    key = lax.broadcasted_iota(jnp.int32, (MOBA_BLOCK, MOBA_BLOCK), 0)
    qry = lax.broadcasted_iota(jnp.int32, (MOBA_BLOCK, MOBA_BLOCK), 1)
    o_ref[0, 0] = _bias_lookup(rb_ref, head, delta * MOBA_BLOCK + qry - key)


def _bias_tiles(rel_bias, nb):
    return pl.pallas_call(
        _bias_tiles_kernel,
        grid=(B_HEADS, nb),
        in_specs=[pl.BlockSpec(memory_space=pltpu.SMEM)],
        out_specs=pl.BlockSpec((1, 1, MOBA_BLOCK, MOBA_BLOCK), lambda h, d: (h, d, 0, 0)),
        out_shape=jax.ShapeDtypeStruct((B_HEADS, nb, MOBA_BLOCK, MOBA_BLOCK), _F32),
        compiler_params=_params("parallel", "parallel"),
    )(rel_bias)


def _moba_prompt_kernel(q_ref, k_ref, v_ref, bias_ref, o_ref, s_scr, *, nb):
    blk = MOBA_BLOCK
    scale = B_DH ** -0.5
    means = jnp.concatenate(
        [jnp.mean(k_ref[c * blk:(c + 1) * blk, :], axis=0, keepdims=True) for c in range(nb)], axis=0)
    gate = lax.dot_general(means, q_ref[...], _NT, preferred_element_type=_F32,
                           precision=lax.Precision.HIGHEST)
    key = lax.broadcasted_iota(jnp.int32, (blk, blk), 0)
    qry = lax.broadcasted_iota(jnp.int32, (blk, blk), 1)
    causal_pen = jnp.where(key <= qry, 0.0, -jnp.inf)
    kb = [k_ref[c * blk:(c + 1) * blk, :].astype(_BF) for c in range(nb)]
    vb = [v_ref[c * blk:(c + 1) * blk, :].astype(_BF) for c in range(nb)]

    slot = 0
    for j in range(nb):
        qb = q_ref[j * blk:(j + 1) * blk, :].astype(_BF)
        g = [gate[c:c + 1, j * blk:(j + 1) * blk] for c in range(j)]
        pens = []
        for c in range(j):
            if j <= MOBA_TOPK:
                pens.append(None)
                continue
            rank = jnp.zeros((1, blk), _F32)
            for c2 in range(j):
                if c2 != c:
                    rank = rank + jnp.where(g[c2] >= g[c] if c2 < c else g[c2] > g[c], 1.0, 0.0)
            pens.append(jnp.where(rank < MOBA_TOPK, 0.0, -jnp.inf))

        m_run = jnp.full((1, blk), -jnp.inf, _F32)
        for c in range(j + 1):
            st = lax.dot_general(kb[c], qb, _NT, preferred_element_type=_F32) * scale + bias_ref[0, j - c]
            if c == j:
                st = st + causal_pen
            elif pens[c] is not None:
                st = st + pens[c]
            s_scr[slot + c] = st
            m_run = jnp.maximum(m_run, jnp.max(st, axis=0, keepdims=True))
        l_run = jnp.zeros((1, blk), _F32)
        acc = jnp.zeros((blk, B_DH), _F32)
        for c in range(j + 1):
            p = jnp.exp(s_scr[slot + c] - m_run)
            l_run = l_run + jnp.sum(p, axis=0, keepdims=True)
            acc = acc + lax.dot_general(p.astype(_BF), vb[c], _TN, preferred_element_type=_F32)
        o_ref[j * blk:(j + 1) * blk, :] = acc / _row_to_col(l_run, blk)
        slot += j + 1


def _moba_prompt(qsrc, k, v, bias, batch, seq):
    n = batch * seq
    nb = seq // MOBA_BLOCK
    per_head = pl.BlockSpec((seq, B_DH), lambda h, b: (b, h))
    return pl.pallas_call(
        functools.partial(_moba_prompt_kernel, nb=nb),
        grid=(B_HEADS, batch),
        in_specs=[per_head, per_head, per_head,
                  pl.BlockSpec((1, nb, MOBA_BLOCK, MOBA_BLOCK), lambda h, b: (h, 0, 0, 0))],
        out_specs=per_head,
        out_shape=jax.ShapeDtypeStruct((n, MIX_TOKEN), _F32),
        scratch_shapes=[pltpu.VMEM((nb * (nb + 1) // 2, MOBA_BLOCK, MOBA_BLOCK), _F32)],
        compiler_params=_params("parallel", "parallel"),
    )(qsrc, k, v, bias)


def _page_means_kernel(pt_ref, ka_ref, kb_ref, o_ref):
    o_ref[0, 0] = (jnp.sum(ka_ref[0], axis=0) + jnp.sum(kb_ref[0], axis=0)) / MOBA_BLOCK


def _page_means(k_pool, page_table, nfull):
    batch = page_table.shape[0]
    assert MOBA_BLOCK == 2 * PAGE_SIZE
    page = lambda which: pl.BlockSpec((1, PAGE_SIZE, B_HEADS, B_DH), lambda b, c, pt: (pt[b, 2 * c + which], 0, 0, 0))
    grid_spec = pltpu.PrefetchScalarGridSpec(
        num_scalar_prefetch=1,
        grid=(batch, nfull),
        in_specs=[page(0), page(1)],
        out_specs=pl.BlockSpec((1, 1, B_HEADS, B_DH), lambda b, c, pt: (b, c, 0, 0)),
    )
    return pl.pallas_call(
        _page_means_kernel,
        grid_spec=grid_spec,
        out_shape=jax.ShapeDtypeStruct((batch, nfull, B_HEADS, B_DH), _F32),
        compiler_params=_params("parallel", "parallel"),
    )(page_table, k_pool, k_pool)


def _moba_sample_select_kernel(q_ref, means_ref, sel_ref, *, seq, nfull):
    lanes = 128
    col = lax.broadcasted_iota(jnp.int32, (seq, lanes), 1)
    colf = col.astype(_F32)
    out = jnp.zeros((seq, lanes), _F32)
    for h in range(B_HEADS):
        qh = q_ref[:, h * B_DH:(h + 1) * B_DH]
        mh = means_ref[0, :, h, :]
        mh = jnp.concatenate([mh, jnp.zeros((lanes - nfull, B_DH), _F32)], axis=0)
        gate = lax.dot_general(qh, mh, _NT, preferred_element_type=_F32, precision=lax.Precision.HIGHEST)
        gate = jnp.where(col < nfull, gate, -jnp.inf)
        for kk in range(MOBA_TOPK):
            best = jnp.max(gate, axis=1, keepdims=True)
            idx = jnp.min(jnp.where(gate == best, colf, float(lanes)), axis=1, keepdims=True)
            out = jnp.where(col == h * MOBA_TOPK + kk, idx, out)
            gate = jnp.where(colf == idx, -jnp.inf, gate)
    sel_ref[0] = out.astype(jnp.int32)


def _moba_sample_select(qsrc, means, batch, seq):
    nfull = means.shape[1]
    assert nfull >= MOBA_TOPK and nfull <= 128 and B_HEADS * MOBA_TOPK <= 128
    return pl.pallas_call(
        functools.partial(_moba_sample_select_kernel, seq=seq, nfull=nfull),
        grid=(batch,),
        in_specs=[
            pl.BlockSpec((seq, MIX_TOKEN), lambda b: (b, 0)),
            pl.BlockSpec((1, nfull, B_HEADS, B_DH), lambda b: (b, 0, 0, 0)),
        ],
        out_specs=pl.BlockSpec((1, seq, 128), lambda b: (b, 0, 0)),
        out_shape=jax.ShapeDtypeStruct((batch, seq, 128), jnp.int32),
        compiler_params=_params("parallel"),
    )(qsrc, means)


def _moba_sample_kernel(sel_ref, pt_ref, q_ref, kn_ref, vn_ref, rb_ref, kpool_ref, vpool_ref, o_ref,
                        kbuf, vbuf, sems, *, seq, past):
    b, h = pl.program_id(0), pl.program_id(1)
    blk = MOBA_BLOCK
    ppb = blk // PAGE_SIZE
    nsel = seq * MOBA_TOPK

    def copies(t, kk, pg):
        block = sel_ref[b, t, h * MOBA_TOPK + kk]
        page = pt_ref[b, block * ppb + pg]
        dst = pl.ds(((t * MOBA_TOPK + kk) * ppb + pg) * PAGE_SIZE, PAGE_SIZE)
        return (pltpu.make_async_copy(kpool_ref.at[page, :, h, :], kbuf.at[dst, :], sems.at[0]),
                pltpu.make_async_copy(vpool_ref.at[page, :, h, :], vbuf.at[dst, :], sems.at[1]))

    every = [(t, kk, pg) for t in range(seq) for kk in range(MOBA_TOPK) for pg in range(ppb)]
    for idx in every:
        for cp in copies(*idx):
            cp.start()

    qf = q_ref[...]
    qb = qf.astype(_BF)
    scale = B_DH ** -0.5
    pad = jnp.zeros((128 - seq, B_DH), _F32)
    k_own = jnp.concatenate([kn_ref[...], pad], axis=0).astype(_BF)
    v_own = jnp.concatenate([vn_ref[...], pad], axis=0).astype(_BF)
    t_own = lax.broadcasted_iota(jnp.int32, (seq, 128), 0)
    s_own = lax.broadcasted_iota(jnp.int32, (seq, 128), 1)
    lo = (lax.dot_general(qb, k_own, _NT, preferred_element_type=_F32) * scale
          + _bias_lookup(rb_ref, h, t_own - s_own))
    lo = jnp.where(s_own <= t_own, lo, -jnp.inf)

    ncol = nsel * blk
    t_sel = lax.broadcasted_iota(jnp.int32, (seq, ncol), 0)
    col = lax.broadcasted_iota(jnp.int32, (seq, ncol), 1)
    slot = col >> int(math.log2(blk))
    block_of_col = jnp.zeros((seq, ncol), jnp.int32)
    token_of_col = jnp.zeros((seq, ncol), jnp.int32)
    for t in range(seq):
        for kk in range(MOBA_TOPK):
            here = slot == t * MOBA_TOPK + kk
            block_of_col = jnp.where(here, sel_ref[b, t, h * MOBA_TOPK + kk], block_of_col)
            token_of_col = jnp.where(here, t, token_of_col)
    dist = past + t_sel - (block_of_col * blk + (col & (blk - 1)))
    bias_sel = _bias_lookup(rb_ref, h, dist)

    for idx in every:
        for cp in copies(*idx):
            cp.wait()

    ls = lax.dot_general(qb, kbuf[...].astype(_BF), _NT, preferred_element_type=_F32) * scale + bias_sel
    ls = jnp.where(token_of_col == t_sel, ls, -jnp.inf)
    m = jnp.maximum(jnp.max(lo, axis=1, keepdims=True), jnp.max(ls, axis=1, keepdims=True))
    p_own = jnp.exp(lo - m)
    p_sel = jnp.exp(ls - m)
    denom = jnp.sum(p_own, axis=1, keepdims=True) + jnp.sum(p_sel, axis=1, keepdims=True)
    out = (jnp.dot(p_own.astype(_BF), v_own, preferred_element_type=_F32)
           + jnp.dot(p_sel.astype(_BF), vbuf[...].astype(_BF), preferred_element_type=_F32))
    o_ref[...] = out / denom


def _moba_sample(qsrc, k_new, v_new, sel, page_table, rel_bias, k_pool, v_pool, batch, seq):
    n = batch * seq
    past = page_table.shape[1] * PAGE_SIZE
    assert past % MOBA_BLOCK == 0 and (past + seq - 1) // MOBA_BLOCK == past // MOBA_BLOCK
    rows = seq * MOBA_TOPK * MOBA_BLOCK
    grid_spec = pltpu.PrefetchScalarGridSpec(
        num_scalar_prefetch=2,
        grid=(batch, B_HEADS),
        in_specs=[
            pl.BlockSpec((seq, B_DH), lambda b, h, *_: (b, h)),
            pl.BlockSpec((seq, B_DH), lambda b, h, *_: (b, h)),
            pl.BlockSpec((seq, B_DH), lambda b, h, *_: (b, h)),
            pl.BlockSpec(memory_space=pltpu.SMEM),
            pl.BlockSpec(memory_space=pl.ANY),
            pl.BlockSpec(memory_space=pl.ANY),
        ],
        out_specs=pl.BlockSpec((seq, B_DH), lambda b, h, *_: (b, h)),
        scratch_shapes=[pltpu.VMEM((rows, B_DH), _F32), pltpu.VMEM((rows, B_DH), _F32),
                        pltpu.SemaphoreType.DMA((2,))],
    )
    return pl.pallas_call(
        functools.partial(_moba_sample_kernel, seq=seq, past=past),
        grid_spec=grid_spec,
        out_shape=jax.ShapeDtypeStruct((n, MIX_TOKEN), _F32),
        compiler_params=_params("arbitrary", "arbitrary"),
    )(sel, page_table, qsrc, k_new, v_new, rel_bias, k_pool, v_pool)


def _top_sorted(scores, count):
    tm = scores.shape[1]
    rank = lax.broadcasted_iota(jnp.int32, (count, tm), 0)
    vals = []
    stacked = jnp.zeros((count, tm), _F32)
    cur = scores
    for r in range(count):
        best = jnp.max(cur, axis=0, keepdims=True)
        vals.append(best)
        stacked = jnp.where(rank == r, best, stacked)
        cur = jnp.where(cur >= best, -jnp.inf, cur)
    return vals, stacked


def _peer_candidates(a, a_st, b, b_st):
    tm = a_st.shape[1]
    half = PEER_TOPK // 2
    groups = [a[0] + b_st[0:half], a[0] + b_st[half:PEER_TOPK]]
    groups += [a[i] + b_st[0:half] for i in range(1, half)]
    groups.append(a_st[half:PEER_TOPK] + b[0])
    return jnp.concatenate([jnp.broadcast_to(g, (half, tm)) for g in groups], axis=0)


def _peer_select_kernel(q_ref, keys_ref, s1_ref, s2_ref, e1_ref, e2_ref, tau_ref):
    tm = q_ref.shape[0]
    for h in range(PEER_HEADS):
        qh = q_ref[:, h * PEER_DKEY:(h + 1) * PEER_DKEY]
        st = lax.dot_general(keys_ref[h], qh, _NT, preferred_element_type=_F32,
                             precision=lax.Precision.HIGHEST)
        s1 = st[0:PEER_NKEYS]
        s2 = st[PEER_NKEYS:2 * PEER_NKEYS]
        a, a_st = _top_sorted(s1, PEER_TOPK)
        bb, b_st = _top_sorted(s2, PEER_TOPK)
        cand = _peer_candidates(a, a_st, bb, b_st)
        tau = jnp.zeros((1, tm), _F32)
        taken = jnp.zeros((1, tm), _F32)
        cur = cand
        for _ in range(PEER_TOPK):
            best = jnp.max(cur, axis=0, keepdims=True)
            hit = cur == best
            tau = jnp.where(taken < PEER_TOPK, best, tau)
            taken = taken + jnp.sum(jnp.where(hit, 1.0, 0.0), axis=0, keepdims=True)
            cur = jnp.where(hit, -jnp.inf, cur)
        top = a[0] + bb[0]
        z = jnp.sum(jnp.where(cand >= tau, jnp.exp(cand - top), 0.0), axis=0, keepdims=True)
        s1_ref[h] = s1
        s2_ref[h] = s2
        e1_ref[h] = jnp.exp(s1 - a[0]) / z
        e2_ref[h] = jnp.exp(s2 - bb[0])
        tau_ref[h] = jnp.broadcast_to(tau, (8, tm))


def _peer_select(q, keys_bd):
    n = q.shape[0]
    tm = min(256, n)
    half = pl.BlockSpec((PEER_HEADS, PEER_NKEYS, tm), lambda i: (0, 0, i))
    half_shape = jax.ShapeDtypeStruct((PEER_HEADS, PEER_NKEYS, n), _F32)
    return pl.pallas_call(
        _peer_select_kernel,
        grid=(n // tm,),
        in_specs=[pl.BlockSpec((tm, PEER_HEADS * PEER_DKEY), lambda i: (i, 0)),
                  pl.BlockSpec((PEER_HEADS, 2 * PEER_NKEYS, PEER_DKEY), lambda i: (0, 0, 0))],
        out_specs=[half, half, half, half, pl.BlockSpec((PEER_HEADS, 8, tm), lambda i: (0, 0, i))],
        out_shape=[half_shape, half_shape, half_shape, half_shape,
                   jax.ShapeDtypeStruct((PEER_HEADS, 8, n), _F32)],
        compiler_params=_params("parallel"),
    )(q, keys_bd)


PEER_TE = 1024
PEER_TM = 512
PEER_TM_GROUP = 512
PEER_SPLIT = 4
PEER_D_CHUNK = 2048


def _peer_dense_kernel(x_ref, u_ref, v_ref, s1_ref, s2_ref, e1_ref, e2_ref, tau_ref, o_ref, *, te, group):
    @pl.when(pl.program_id(1) == 0)
    def _():
        o_ref[...] = jnp.zeros_like(o_ref)

    tm = x_ref.shape[0]
    sub = te // PEER_SPLIT
    rows = sub // PEER_NKEYS
    for g0 in range(0, tm, group):
        cols = slice(g0, g0 + group)
        xg = x_ref[cols, :]
        a_t = [lax.dot_general(u_ref[s * sub:(s + 1) * sub, :], xg, _NT, preferred_element_type=_F32)
               for s in range(PEER_SPLIT)]
        for s in range(PEER_SPLIT):
            parts = []
            for r in range(rows):
                k1 = s * rows + r
                w = jnp.zeros((PEER_NKEYS, group), _F32)
                for h in range(PEER_HEADS):
                    total = s1_ref[h, k1:k1 + 1, cols] + s2_ref[h, :, cols]
                    w = w + jnp.where(total >= tau_ref[h, 0:1, cols],
                                      e2_ref[h, :, cols] * e1_ref[h, k1:k1 + 1, cols], 0.0)
                act = jax.nn.gelu(a_t[s][r * PEER_NKEYS:(r + 1) * PEER_NKEYS])
                parts.append((act * w).astype(_BF))
            gated = jnp.concatenate(parts, axis=0)
            for d0 in range(0, o_ref.shape[0], PEER_D_CHUNK):
                o_ref[d0:d0 + PEER_D_CHUNK, cols] += lax.dot_general(
                    v_ref[s * sub:(s + 1) * sub, d0:d0 + PEER_D_CHUNK], gated, _TN, preferred_element_type=_F32)


def _peer_dense(xb, u_stack, v_stack, layer, sel):
    n, d = xb.shape
    tm = min(PEER_TM, n)
    te = PEER_TE
    rows = te // PEER_NKEYS
    assert rows == 8
    held = pl.Buffered(1)
    per_key2 = pl.BlockSpec((PEER_HEADS, PEER_NKEYS, tm), lambda i, e: (0, 0, i), pipeline_mode=held)
    per_key1 = pl.BlockSpec((PEER_HEADS, rows, tm), lambda i, e: (0, e, i))
    s1, s2, e1, e2, tau = sel
    return pl.pallas_call(
        functools.partial(_peer_dense_kernel, te=te, group=min(PEER_TM_GROUP, tm)),
        grid=(n // tm, PEER_N // te),
        in_specs=[pl.BlockSpec((tm, d), lambda i, e: (i, 0), pipeline_mode=held),
                  pl.BlockSpec((None, te, d), lambda i, e: (layer, e, 0)),
                  pl.BlockSpec((None, te, d), lambda i, e: (layer, e, 0)),
                  per_key1, per_key2, per_key1, per_key2,
                  pl.BlockSpec((PEER_HEADS, 8, tm), lambda i, e: (0, 0, i), pipeline_mode=held)],
        out_specs=pl.BlockSpec((d, tm), lambda i, e: (0, i), pipeline_mode=held),
        out_shape=jax.ShapeDtypeStruct((d, n), _F32),
        compiler_params=_params("parallel", "arbitrary"),
    )(xb, u_stack, v_stack, s1, s2, e1, e2, tau)


def _peer(xb, layer, wq_stack, keys_bd, u_stack, v_stack):
    n = xb.shape[0]
    n_pad = max(n, 128)
    if n_pad != n:
        xb = jnp.pad(xb, ((0, n_pad - n), (0, 0)))
    q = _matmul([xb], [(wq_stack, layer, 0)], 1024, 512)
    return _peer_dense(xb, u_stack, v_stack, layer, _peer_select(q, keys_bd))


def _layer_tail(x, tok, mem, layer, w):
    n = x.shape[0]
    mixed = _matmul([tok, mem], [(w['out'], layer, 0), (w['out'], layer, MIX_TOKEN)], 512, 512)
    x1, x1b = _add_layernorm(x, mixed, *w['ln1'][layer])
    ffn_t = _peer(x1b, layer, w['peer_wq'], w['peer_keys'][layer], w['peer_u'], w['peer_v'])
    if ffn_t.shape[1] == n:
        return _add_layernorm(x1, ffn_t, *w['ln2'][layer], y_transposed=True)
    return _add_layernorm(x1, ffn_t[:, :n].T, *w['ln2'][layer])


def _trunk(x, batch, seq, mem_kv, mlstm_init, moba_attend, w):
    xb = x.astype(_BF)
    u = _matmul_nt(xb, w['in_a_main'], 1024, 512)
    qm = _matmul_nt(xb, w['in_a_mem'], 1024, 512)
    gates = _matmul_nt(x, w['in_a_gates'], 512, 128, precision=lax.Precision.HIGHEST)
    tok, state = _mlstm(u, gates[:, :2 * M_HEADS].T, w['b_if'], *mlstm_init, batch, seq)
    mem = _mem_attend(qm, 0, *mem_kv[0], batch, seq)
    x, xb = _layer_tail(x, tok, mem, 0, w)
    k_sh = _matmul([xb], [w['k_shared']], 1024, 512)
    v_sh = _matmul([xb], [w['v_shared']], 1024, 512)
    ub = _matmul([xb], [w['in_b']], 1024, 512)
    tok = moba_attend(ub, k_sh, v_sh)
    mem = _mem_attend(ub, MIX_TOKEN, *mem_kv[1], batch, seq)
    x, _ = _layer_tail(x, tok, mem, 1, w)
    return x, state, k_sh, v_sh


def kernel(x_prompt, x_sample, cache_moba_k, cache_moba_v, cache_mem_k, cache_mem_v, state_mlstm_c,
           state_mlstm_n, state_mlstm_m, page_table, mem_prompt, w_in_a, b_if_a, w_in_b, w_kv_shared, rel_bias,
           w_mem_kv, w_out, ln1_g, ln1_b, ln2_g, ln2_b, peer_wq, peer_keys, peer_u, peer_v):
    bp, tp, d = x_prompt.shape
    bs, ts, _ = x_sample.shape
    gate0 = 2 * M_HEADS * M_DQK + 2 * MIX_TOKEN
    wa = w_in_a[0]
    zeros = jnp.zeros((PEER_HEADS, PEER_NKEYS, PEER_DKEY // 2), _F32)

    def keys_blockdiag(keys):
        return jnp.concatenate([jnp.concatenate([keys[:, 0], zeros], axis=2),
                                jnp.concatenate([zeros, keys[:, 1]], axis=2)], axis=1)

    w = {
        'in_a_main': wa[:, :gate0].T.astype(_BF),
        'in_a_mem': wa[:, gate0 + 2 * M_HEADS:].T.astype(_BF),
        'in_a_gates': jnp.pad(wa[:, gate0:gate0 + 2 * M_HEADS].T, ((0, 128 - 2 * M_HEADS), (0, 0))),
        'b_if': b_if_a[0],
        'in_b': w_in_b[0].astype(_BF),
        'k_shared': w_kv_shared[:, :MIX_TOKEN].astype(_BF),
        'v_shared': w_kv_shared[:, MIX_TOKEN:].astype(_BF),
        'out': w_out.astype(_BF),
        'ln1': [(ln1_g[l], ln1_b[l]) for l in range(DEPTH)],
        'ln2': [(ln2_g[l], ln2_b[l]) for l in range(DEPTH)],
        'peer_wq': peer_wq.astype(_BF),
        'peer_keys': [keys_blockdiag(peer_keys[l]) for l in range(DEPTH)],
        'peer_u': peer_u.astype(_BF),
        'peer_v': peer_v.astype(_BF),
    }

    mem_w = jnp.concatenate([w_mem_kv[l] for l in range(DEPTH)], axis=1).astype(_BF)
    mkv = _matmul([mem_prompt.reshape(bp * MEM_SLOTS, d).astype(_BF)], [mem_w], 1024, 512)
    mkv3 = mkv.reshape(bp, MEM_SLOTS, 2 * DEPTH * MIX_MEM)
    mem_kv_p = [(mkv3, mkv3, 2 * l * MIX_MEM, (2 * l + 1) * MIX_MEM) for l in range(DEPTH)]
    init_p = (jnp.zeros((bp, M_HEADS, M_DQK, M_DV), _F32), jnp.zeros((bp, M_HEADS, M_DQK), _F32),
              jnp.zeros((bp, M_HEADS), _F32))
    bias = _bias_tiles(rel_bias, tp // MOBA_BLOCK)
    y_p, st_p, k_p, v_p = _trunk(
        x_prompt.reshape(bp * tp, d), bp, tp, mem_kv_p, init_p,
        lambda ub, k_sh, v_sh: _moba_prompt(ub, k_sh, v_sh, bias, bp, tp), w)

    mem_kv_s = [(cache_mem_k[l].reshape(bs, MEM_SLOTS, MIX_MEM), cache_mem_v[l].reshape(bs, MEM_SLOTS, MIX_MEM), 0, 0)
                for l in range(DEPTH)]
    init_s = (state_mlstm_c[0], state_mlstm_n[0], state_mlstm_m[0])
    nfull = page_table.shape[1] * PAGE_SIZE // MOBA_BLOCK

    def moba_sample(ub, k_sh, v_sh):
        means = _page_means(cache_moba_k, page_table, nfull)
        sel = _moba_sample_select(ub, means, bs, ts)
        return _moba_sample(ub, k_sh, v_sh, sel, page_table, rel_bias, cache_moba_k, cache_moba_v, bs, ts)

    y_s, st_s, k_s, v_s = _trunk(x_sample.reshape(bs * ts, d), bs, ts, mem_kv_s, init_s, moba_sample, w)

    mem_k_p = jnp.stack([mkv3[:, :, 2 * l * MIX_MEM:(2 * l + 1) * MIX_MEM] for l in range(DEPTH)])
    mem_v_p = jnp.stack([mkv3[:, :, (2 * l + 1) * MIX_MEM:(2 * l + 2) * MIX_MEM] for l in range(DEPTH)])
    kv_shape = (bp, MEM_SLOTS, MEM_HEADS, MEM_DH)
    return (y_p.reshape(bp, tp, d), y_s.reshape(bs, ts, d),
            st_p[0][None], st_p[1][None], st_p[2][None],
            k_p.reshape(bp, tp, B_HEADS, B_DH), v_p.reshape(bp, tp, B_HEADS, B_DH),
            mem_k_p.reshape((DEPTH,) + kv_shape), mem_v_p.reshape((DEPTH,) + kv_shape),
            st_s[0][None], st_s[1][None], st_s[2][None],
            k_s.reshape(bs, ts, B_HEADS, B_DH), v_s.reshape(bs, ts, B_HEADS, B_DH))
```

```python
import functools
import math

import jax
import jax.numpy as jnp
from jax import lax
from jax.experimental import pallas as pl
from jax.experimental.pallas import tpu as pltpu

D_MODEL = 4096
DEPTH = 2
PAGE_SIZE = 128
MIX_TOKEN = 3 * D_MODEL // 4
MIX_MEM = D_MODEL // 4
M_HEADS = 6
M_DV = MIX_TOKEN // M_HEADS
M_DQK = M_DV // 2
M_CHUNK_MAX = 256
M_CHUNK_MIN = 128
B_HEADS = 24
B_DH = MIX_TOKEN // B_HEADS
MOBA_BLOCK = 256
MOBA_TOPK = 3
MEM_SLOTS = 256
MEM_HEADS = 4
MEM_DH = MIX_MEM // MEM_HEADS
REL_BUCKETS = 32
REL_MAX_DIST = 4096
PEER_HEADS = 8
PEER_NKEYS = 128
PEER_N = PEER_NKEYS * PEER_NKEYS
PEER_DKEY = 128
PEER_TOPK = 16
DN_ALPHA = (2.0 * DEPTH) ** 0.25
LN_EPS = 1e-5

VMEM_LIMIT_V7X = 56 * 1024 * 1024
NEG_BIG = -1e30

_NT = (((1,), (1,)), ((), ()))
_TN = (((0,), (0,)), ((), ()))
_BF = jnp.bfloat16
_F32 = jnp.float32


def _params(*sem):
    return pltpu.CompilerParams(dimension_semantics=sem, vmem_limit_bytes=VMEM_LIMIT_V7X)


def _row_to_col(row, n):
    eye = lax.broadcasted_iota(jnp.int32, (n, n), 0) == lax.broadcasted_iota(jnp.int32, (n, n), 1)
    return jnp.sum(jnp.where(eye, jnp.broadcast_to(row, (n, n)), 0.0), axis=1, keepdims=True)


def _col_to_row(col, n):
    eye = lax.broadcasted_iota(jnp.int32, (n, n), 0) == lax.broadcasted_iota(jnp.int32, (n, n), 1)
    return jnp.sum(jnp.where(eye, jnp.broadcast_to(col, (n, n)), 0.0), axis=0, keepdims=True)


def _mm_kernel(*refs, n_pairs, precision):
    o_ref = refs[2 * n_pairs]
    acc = None
    for a_ref, w_ref in zip(refs[:n_pairs], refs[n_pairs:2 * n_pairs]):
        a = a_ref[...]
        if precision is None:
            a = a.astype(_BF)
        d = jnp.dot(a, w_ref[...], preferred_element_type=_F32, precision=precision)
        acc = d if acc is None else acc + d
    o_ref[...] = acc


def _weight_spec(w, k, tn):
    if not isinstance(w, tuple):
        return w, pl.BlockSpec((k, tn), lambda i, j: (0, j))
    stack, layer, row0 = w
    assert row0 % k == 0
    return stack, pl.BlockSpec((None, k, tn), lambda i, j: (layer, row0 // k, j))


def _matmul(a_list, w_list, tm, tn, precision=None):
    n = a_list[0].shape[0]
    w0 = w_list[0]
    n_out = (w0[0] if isinstance(w0, tuple) else w0).shape[-1]
    tm = min(tm, n)
    tn = min(tn, n_out)
    views = [_weight_spec(w, a.shape[1], tn) for a, w in zip(a_list, w_list)]
    in_specs = [pl.BlockSpec((tm, a.shape[1]), lambda i, j: (i, 0)) for a in a_list]
    in_specs += [spec for _, spec in views]
    return pl.pallas_call(
        functools.partial(_mm_kernel, n_pairs=len(a_list), precision=precision),
        grid=(n // tm, n_out // tn),
        in_specs=in_specs,
        out_specs=pl.BlockSpec((tm, tn), lambda i, j: (i, j)),
        out_shape=jax.ShapeDtypeStruct((n, n_out), _F32),
        compiler_params=_params("parallel", "parallel"),
    )(*a_list, *[arr for arr, _ in views])


def _mm_nt_kernel(a_ref, wt_ref, o_ref, *, precision):
    a = a_ref[...] if precision is not None else a_ref[...].astype(_BF)
    o_ref[...] = lax.dot_general(a, wt_ref[...], _NT, preferred_element_type=_F32, precision=precision)


def _matmul_nt(a, w_t, tm, tn, precision=None):
    n, k = a.shape
    n_out = w_t.shape[0]
    tm = min(tm, n)
    tn = min(tn, n_out)
    return pl.pallas_call(
        functools.partial(_mm_nt_kernel, precision=precision),
        grid=(n // tm, n_out // tn),
        in_specs=[pl.BlockSpec((tm, k), lambda i, j: (i, 0)), pl.BlockSpec((tn, k), lambda i, j: (j, 0))],
        out_specs=pl.BlockSpec((tm, tn), lambda i, j: (i, j)),
        out_shape=jax.ShapeDtypeStruct((n, n_out), _F32),
        compiler_params=_params("parallel", "parallel"),
    )(a, w_t)


def _ln_kernel(x_ref, y_ref, g_ref, b_ref, o_ref, ob_ref, *, y_transposed):
    y = y_ref[...].T if y_transposed else y_ref[...]
    z = DN_ALPHA * x_ref[...] + y
    mu = jnp.mean(z, axis=-1, keepdims=True)
    zc = z - mu
    var = jnp.mean(zc * zc, axis=-1, keepdims=True)
    y = zc * lax.rsqrt(var + LN_EPS) * g_ref[...] + b_ref[...]
    o_ref[...] = y
    ob_ref[...] = y.astype(_BF)


def _add_layernorm(x, y, g, b, y_transposed=False):
    n, d = x.shape
    tm = min(128, n)
    row = pl.BlockSpec((tm, d), lambda i: (i, 0))
    vec = pl.BlockSpec((1, d), lambda i: (0, 0))
    y_spec = pl.BlockSpec((d, tm), lambda i: (0, i)) if y_transposed else row
    return pl.pallas_call(
        functools.partial(_ln_kernel, y_transposed=y_transposed),
        grid=(n // tm,),
        in_specs=[row, y_spec, vec, vec],
        out_specs=[row, row],
        out_shape=[jax.ShapeDtypeStruct((n, d), _F32), jax.ShapeDtypeStruct((n, d), _BF)],
        compiler_params=_params("parallel"),
    )(x, y, g.reshape(1, d), b.reshape(1, d))


def _log_sigmoid(x):
    return jnp.minimum(x, 0.0) - jnp.log1p(jnp.exp(-jnp.abs(x)))


def _mlstm_kernel(bif_ref, q_ref, k_ref, v_ref, og_ref, gi_ref, gf_ref, c0_ref, n0_ref, m0_ref,
                  tok_ref, c_ref, n_ref, m_ref, *, l_in, l):
    h = pl.program_id(1)

    @pl.when(pl.program_id(2) == 0)
    def _():
        c_ref[...] = c0_ref[...]
        n_ref[...] = n0_ref[...]
        m_ref[...] = m0_ref[...]

    def rows(ref):
        x = ref[...]
        if l_in == l:
            return x
        return jnp.concatenate([x, jnp.zeros((l - l_in, x.shape[1]), x.dtype)], axis=0)

    qf = rows(q_ref)
    kf = rows(k_ref) * (M_DQK ** -0.5)
    vb = rows(v_ref).astype(_BF)
    i_row = gi_ref[0, 0] + bif_ref[h]
    f_row = gf_ref[0, 0] + bif_ref[M_HEADS + h]
    lf_row = _log_sigmoid(f_row)

    t_idx = lax.broadcasted_iota(jnp.int32, (l, l), 0)
    s_idx = lax.broadcasted_iota(jnp.int32, (l, l), 1)
    causal = s_idx <= t_idx
    b_col = jnp.sum(jnp.where(causal, jnp.broadcast_to(lf_row, (l, l)), 0.0), axis=1, keepdims=True)
    b_row = _col_to_row(b_col, l)
    i_col = _row_to_col(i_row, l)

    m_prev = m_ref[0, 0, :, 0:1]
    dmat = jnp.where(causal, b_col - b_row + i_row, -jnp.inf)
    inter = b_col + m_prev
    m_t = jnp.maximum(inter, jnp.max(dmat, axis=1, keepdims=True))
    w_inter = jnp.exp(inter - m_t)

    qb = qf.astype(_BF)
    kb = kf.astype(_BF)
    s = lax.dot_general(qb, kb, _NT, preferred_element_type=_F32) * jnp.exp(dmat - m_t)
    c_old = c_ref[0, 0]
    n_old = n_ref[0, 0]
    num = (jnp.dot(s.astype(_BF), vb, preferred_element_type=_F32)
           + w_inter * jnp.dot(qb, c_old.astype(_BF), preferred_element_type=_F32))
    den = jnp.sum(s, axis=1, keepdims=True) + w_inter * jnp.sum(qf * n_old, axis=1, keepdims=True)
    hid = num / jnp.maximum(jnp.abs(den), jnp.exp(-m_t))
    tok = jax.nn.sigmoid(rows(og_ref)) * hid
    tok_ref[...] = tok[0:l_in]

    m_new = m_t[l - 1:l, :]
    b_last = b_col[l - 1:l, :]
    w_end = jnp.exp(b_last - b_col + i_col - m_new)
    decay = jnp.exp(b_last + m_prev - m_new)
    kw = kf * w_end
    c_ref[0, 0] = decay * c_old + lax.dot_general(kw.astype(_BF), vb, _TN, preferred_element_type=_F32)
    n_ref[0, 0] = decay * n_old + jnp.sum(kw, axis=0, keepdims=True)
    m_ref[0, 0] = jnp.broadcast_to(m_new, (1, 128))


def _mlstm(u, gates_t, b_if, c0, n0, m0, batch, seq):
    n = batch * seq
    l = min(seq, M_CHUNK_MAX)
    nc = seq // l
    l_pad = max(l, M_CHUNK_MIN)
    gi = gates_t[:M_HEADS].reshape(M_HEADS, n // l, 1, l)
    gf = gates_t[M_HEADS:].reshape(M_HEADS, n // l, 1, l)
    if l_pad != l:
        gi = jnp.pad(gi, ((0, 0), (0, 0), (0, 0), (0, l_pad - l)), constant_values=NEG_BIG)
        gf = jnp.pad(gf, ((0, 0), (0, 0), (0, 0), (0, l_pad - l)), constant_values=-NEG_BIG)
    vq = M_DV // M_DQK
    row = lambda b, h, c: b * nc + c
    state_c = pl.BlockSpec((1, 1, M_DQK, M_DV), lambda b, h, c: (b, h, 0, 0))
    state_n = pl.BlockSpec((1, 1, 1, M_DQK), lambda b, h, c: (b, h, 0, 0))
    state_m = pl.BlockSpec((1, 1, 1, 128), lambda b, h, c: (b, h, 0, 0))
    gate = pl.BlockSpec((1, 1, 1, l_pad), lambda b, h, c: (h, row(b, h, c), 0, 0))
    tok, c_new, n_new, m_new = pl.pallas_call(
        functools.partial(_mlstm_kernel, l_in=l, l=l_pad),
        grid=(batch, M_HEADS, nc),
        in_specs=[
            pl.BlockSpec(memory_space=pltpu.SMEM),
            pl.BlockSpec((l, M_DQK), lambda b, h, c: (row(b, h, c), h)),
            pl.BlockSpec((l, M_DQK), lambda b, h, c: (row(b, h, c), M_HEADS + h)),
            pl.BlockSpec((l, M_DV), lambda b, h, c: (row(b, h, c), M_HEADS + h)),
            pl.BlockSpec((l, M_DV), lambda b, h, c: (row(b, h, c), 2 * M_HEADS + h)),
            gate, gate, state_c, state_n, state_m,
        ],
        out_specs=[pl.BlockSpec((l, M_DV), lambda b, h, c: (row(b, h, c), h)), state_c, state_n, state_m],
        out_shape=[
            jax.ShapeDtypeStruct((n, MIX_TOKEN), _F32),
            jax.ShapeDtypeStruct((batch, M_HEADS, M_DQK, M_DV), _F32),
            jax.ShapeDtypeStruct((batch, M_HEADS, 1, M_DQK), _F32),
            jax.ShapeDtypeStruct((batch, M_HEADS, 1, 128), _F32),
        ],
        compiler_params=_params("parallel", "parallel", "arbitrary"),
    )(b_if, u, u, u, u, gi, gf, c0, n0.reshape(batch, M_HEADS, 1, M_DQK),
      jnp.broadcast_to(m0[:, :, None, None], (batch, M_HEADS, 1, 128)))
    assert vq * M_DQK == M_DV
    return tok, (c_new, n_new.reshape(batch, M_HEADS, M_DQK), m_new[:, :, 0, 0])


def _memattn_kernel(q_ref, k_ref, v_ref, o_ref):
    q = q_ref[...].astype(_BF)
    logits = lax.dot_general(q, k_ref[0].astype(_BF), _NT, preferred_element_type=_F32) * (MEM_DH ** -0.5)
    e = jnp.exp(logits - jnp.max(logits, axis=-1, keepdims=True))
    p = e / jnp.sum(e, axis=-1, keepdims=True)
    o_ref[...] = jnp.dot(p.astype(_BF), v_ref[0].astype(_BF), preferred_element_type=_F32)


def _mem_attend(qsrc, q_col0, mk, mv, k_col0, v_col0, batch, seq):
    n = batch * seq
    tq = min(seq, 512)
    nq = seq // tq
    qc, kc, vc = q_col0 // MEM_DH, k_col0 // MEM_DH, v_col0 // MEM_DH
    return pl.pallas_call(
        _memattn_kernel,
        grid=(batch, nq, MEM_HEADS),
        in_specs=[
            pl.BlockSpec((tq, MEM_DH), lambda b, t, h: (b * nq + t, qc + h)),
            pl.BlockSpec((1, MEM_SLOTS, MEM_DH), lambda b, t, h: (b, 0, kc + h)),
            pl.BlockSpec((1, MEM_SLOTS, MEM_DH), lambda b, t, h: (b, 0, vc + h)),
        ],
        out_specs=pl.BlockSpec((tq, MEM_DH), lambda b, t, h: (b * nq + t, h)),
        out_shape=jax.ShapeDtypeStruct((n, MIX_MEM), _F32),
        compiler_params=_params("parallel", "parallel", "parallel"),
    )(qsrc, mk, mv)


def _rel_bucket(dist):
    n = jnp.maximum(dist, 0)
    max_exact = REL_BUCKETS // 2
    nf = jnp.maximum(n, 1).astype(_F32)
    large = max_exact + (jnp.log(nf / max_exact) / math.log(REL_MAX_DIST / max_exact)
                         * (REL_BUCKETS - max_exact)).astype(jnp.int32)
    return jnp.where(n < max_exact, n, jnp.minimum(large, REL_BUCKETS - 1))


def _bias_lookup(rb_ref, head, dist):
    bucket = _rel_bucket(dist)
    val = jnp.full(dist.shape, rb_ref[0, head], _F32)
    for b in range(1, REL_BUCKETS):
        val = jnp.where(bucket == b, rb_ref[b, head], val)
    return val


def _bias_tiles_kernel(rb_ref, o_ref):
    head, delta = pl.program_id(0), pl.program_id(1)
    key = lax.broadcasted_iota(jnp.int32, (MOBA_BLOCK, MOBA_BLOCK), 0)
    qry = lax.broadcasted_iota(jnp.int32, (MOBA_BLOCK, MOBA_BLOCK), 1)
    o_ref[0, 0] = _bias_lookup(rb_ref, head, delta * MOBA_BLOCK + qry - key)


def _bias_tiles(rel_bias, nb):
    return pl.pallas_call(
        _bias_tiles_kernel,
        grid=(B_HEADS, nb),
        in_specs=[pl.BlockSpec(memory_space=pltpu.SMEM)],
        out_specs=pl.BlockSpec((1, 1, MOBA_BLOCK, MOBA_BLOCK), lambda h, d: (h, d, 0, 0)),
        out_shape=jax.ShapeDtypeStruct((B_HEADS, nb, MOBA_BLOCK, MOBA_BLOCK), _F32),
        compiler_params=_params("parallel", "parallel"),
    )(rel_bias)


def _moba_prompt_kernel(q_ref, k_ref, v_ref, bias_ref, o_ref, s_scr, *, nb):
    blk = MOBA_BLOCK
    scale = B_DH ** -0.5
    means = jnp.concatenate(
        [jnp.mean(k_ref[c * blk:(c + 1) * blk, :], axis=0, keepdims=True) for c in range(nb)], axis=0)
    gate = lax.dot_general(means, q_ref[...], _NT, preferred_element_type=_F32,
                           precision=lax.Precision.HIGHEST)
    key = lax.broadcasted_iota(jnp.int32, (blk, blk), 0)
    qry = lax.broadcasted_iota(jnp.int32, (blk, blk), 1)
    causal_pen = jnp.where(key <= qry, 0.0, -jnp.inf)
    kb = [k_ref[c * blk:(c + 1) * blk, :].astype(_BF) for c in range(nb)]
    vb = [v_ref[c * blk:(c + 1) * blk, :].astype(_BF) for c in range(nb)]

    slot = 0
    for j in range(nb):
        qb = q_ref[j * blk:(j + 1) * blk, :].astype(_BF)
        g = [gate[c:c + 1, j * blk:(j + 1) * blk] for c in range(j)]
        pens = []
        for c in range(j):
            if j <= MOBA_TOPK:
                pens.append(None)
                continue
            rank = jnp.zeros((1, blk), _F32)
            for c2 in range(j):
                if c2 != c:
                    rank = rank + jnp.where(g[c2] >= g[c] if c2 < c else g[c2] > g[c], 1.0, 0.0)
            pens.append(jnp.where(rank < MOBA_TOPK, 0.0, -jnp.inf))

        m_run = jnp.full((1, blk), -jnp.inf, _F32)
        for c in range(j + 1):
            st = lax.dot_general(kb[c], qb, _NT, preferred_element_type=_F32) * scale + bias_ref[0, j - c]
            if c == j:
                st = st + causal_pen
            elif pens[c] is not None:
                st = st + pens[c]
            s_scr[slot + c] = st
            m_run = jnp.maximum(m_run, jnp.max(st, axis=0, keepdims=True))
        l_run = jnp.zeros((1, blk), _F32)
        acc = jnp.zeros((blk, B_DH), _F32)
        for c in range(j + 1):
            p = jnp.exp(s_scr[slot + c] - m_run)
            l_run = l_run + jnp.sum(p, axis=0, keepdims=True)
            acc = acc + lax.dot_general(p.astype(_BF), vb[c], _TN, preferred_element_type=_F32)
        o_ref[j * blk:(j + 1) * blk, :] = acc / _row_to_col(l_run, blk)
        slot += j + 1


def _moba_prompt(qsrc, k, v, bias, batch, seq):
    n = batch * seq
    nb = seq // MOBA_BLOCK
    per_head = pl.BlockSpec((seq, B_DH), lambda h, b: (b, h))
    return pl.pallas_call(
        functools.partial(_moba_prompt_kernel, nb=nb),
        grid=(B_HEADS, batch),
        in_specs=[per_head, per_head, per_head,
                  pl.BlockSpec((1, nb, MOBA_BLOCK, MOBA_BLOCK), lambda h, b: (h, 0, 0, 0))],
        out_specs=per_head,
        out_shape=jax.ShapeDtypeStruct((n, MIX_TOKEN), _F32),
        scratch_shapes=[pltpu.VMEM((nb * (nb + 1) // 2, MOBA_BLOCK, MOBA_BLOCK), _F32)],
        compiler_params=_params("parallel", "parallel"),
    )(qsrc, k, v, bias)


def _page_means_kernel(pt_ref, ka_ref, kb_ref, o_ref):
    o_ref[0, 0] = (jnp.sum(ka_ref[0], axis=0) + jnp.sum(kb_ref[0], axis=0)) / MOBA_BLOCK


def _page_means(k_pool, page_table, nfull):
    batch = page_table.shape[0]
    assert MOBA_BLOCK == 2 * PAGE_SIZE
    page = lambda which: pl.BlockSpec((1, PAGE_SIZE, B_HEADS, B_DH), lambda b, c, pt: (pt[b, 2 * c + which], 0, 0, 0))
    grid_spec = pltpu.PrefetchScalarGridSpec(
        num_scalar_prefetch=1,
        grid=(batch, nfull),
        in_specs=[page(0), page(1)],
        out_specs=pl.BlockSpec((1, 1, B_HEADS, B_DH), lambda b, c, pt: (b, c, 0, 0)),
    )
    return pl.pallas_call(
        _page_means_kernel,
        grid_spec=grid_spec,
        out_shape=jax.ShapeDtypeStruct((batch, nfull, B_HEADS, B_DH), _F32),
        compiler_params=_params("parallel", "parallel"),
    )(page_table, k_pool, k_pool)


def _moba_sample_select_kernel(q_ref, means_ref, sel_ref, *, seq, nfull):
    lanes = 128
    col = lax.broadcasted_iota(jnp.int32, (seq, lanes), 1)
    colf = col.astype(_F32)
    out = jnp.zeros((seq, lanes), _F32)
    for h in range(B_HEADS):
        qh = q_ref[:, h * B_DH:(h + 1) * B_DH]
        mh = means_ref[0, :, h, :]
        mh = jnp.concatenate([mh, jnp.zeros((lanes - nfull, B_DH), _F32)], axis=0)
        gate = lax.dot_general(qh, mh, _NT, preferred_element_type=_F32, precision=lax.Precision.HIGHEST)
        gate = jnp.where(col < nfull, gate, -jnp.inf)
        for kk in range(MOBA_TOPK):
            best = jnp.max(gate, axis=1, keepdims=True)
            idx = jnp.min(jnp.where(gate == best, colf, float(lanes)), axis=1, keepdims=True)
            out = jnp.where(col == h * MOBA_TOPK + kk, idx, out)
            gate = jnp.where(colf == idx, -jnp.inf, gate)
    sel_ref[0] = out.astype(jnp.int32)


def _moba_sample_select(qsrc, means, batch, seq):
    nfull = means.shape[1]
    assert nfull >= MOBA_TOPK and nfull <= 128 and B_HEADS * MOBA_TOPK <= 128
    return pl.pallas_call(
        functools.partial(_moba_sample_select_kernel, seq=seq, nfull=nfull),
        grid=(batch,),
        in_specs=[
            pl.BlockSpec((seq, MIX_TOKEN), lambda b: (b, 0)),
            pl.BlockSpec((1, nfull, B_HEADS, B_DH), lambda b: (b, 0, 0, 0)),
        ],
        out_specs=pl.BlockSpec((1, seq, 128), lambda b: (b, 0, 0)),
        out_shape=jax.ShapeDtypeStruct((batch, seq, 128), jnp.int32),
        compiler_params=_params("parallel"),
    )(qsrc, means)


def _moba_sample_kernel(sel_ref, pt_ref, q_ref, kn_ref, vn_ref, rb_ref, kpool_ref, vpool_ref, o_ref,
                        kbuf, vbuf, sems, *, seq, past):
    b, h = pl.program_id(0), pl.program_id(1)
    n_heads = pl.num_programs(1)
    step = b * n_heads + h
    cur = step % 2
    blk = MOBA_BLOCK
    ppb = blk // PAGE_SIZE
    nsel = seq * MOBA_TOPK

    def copies(bb, hh, half, t, kk, pg):
        block = sel_ref[bb, t, hh * MOBA_TOPK + kk]
        page = pt_ref[bb, block * ppb + pg]
        dst = pl.ds(((t * MOBA_TOPK + kk) * ppb + pg) * PAGE_SIZE, PAGE_SIZE)
        return (pltpu.make_async_copy(kpool_ref.at[page, :, hh, :], kbuf.at[half, dst, :], sems.at[0, half]),
                pltpu.make_async_copy(vpool_ref.at[page, :, hh, :], vbuf.at[half, dst, :], sems.at[1, half]))

    every = [(t, kk, pg) for t in range(seq) for kk in range(MOBA_TOPK) for pg in range(ppb)]

    def start_all(bb, hh, half):
        for idx in every:
            for cp in copies(bb, hh, half, *idx):
                cp.start()

    @pl.when(step == 0)
    def _():
        start_all(b, h, cur)

    @pl.when(step + 1 < pl.num_programs(0) * n_heads)
    def _():
        wrap = h + 1 == n_heads
        start_all(jnp.where(wrap, b + 1, b), jnp.where(wrap, 0, h + 1), 1 - cur)

    qf = q_ref[...]
    qb = qf.astype(_BF)
    scale = B_DH ** -0.5
    pad = jnp.zeros((128 - seq, B_DH), _F32)
    k_own = jnp.concatenate([kn_ref[...], pad], axis=0).astype(_BF)
    v_own = jnp.concatenate([vn_ref[...], pad], axis=0).astype(_BF)
    t_own = lax.broadcasted_iota(jnp.int32, (seq, 128), 0)
    s_own = lax.broadcasted_iota(jnp.int32, (seq, 128), 1)
    lo = (lax.dot_general(qb, k_own, _NT, preferred_element_type=_F32) * scale
          + _bias_lookup(rb_ref, h, t_own - s_own))
    lo = jnp.where(s_own <= t_own, lo, -jnp.inf)

    ncol = nsel * blk
    t_sel = lax.broadcasted_iota(jnp.int32, (seq, ncol), 0)
    col = lax.broadcasted_iota(jnp.int32, (seq, ncol), 1)
    slot = col >> int(math.log2(blk))
    block_of_col = jnp.zeros((seq, ncol), jnp.int32)
    token_of_col = jnp.zeros((seq, ncol), jnp.int32)
    for t in range(seq):
        for kk in range(MOBA_TOPK):
            here = slot == t * MOBA_TOPK + kk
            block_of_col = jnp.where(here, sel_ref[b, t, h * MOBA_TOPK + kk], block_of_col)
            token_of_col = jnp.where(here, t, token_of_col)
    dist = past + t_sel - (block_of_col * blk + (col & (blk - 1)))
    bias_sel = _bias_lookup(rb_ref, h, dist)

    for idx in every:
        for cp in copies(b, h, cur, *idx):
            cp.wait()

    ls = lax.dot_general(qb, kbuf[cur].astype(_BF), _NT, preferred_element_type=_F32) * scale + bias_sel
    ls = jnp.where(token_of_col == t_sel, ls, -jnp.inf)
    m = jnp.maximum(jnp.max(lo, axis=1, keepdims=True), jnp.max(ls, axis=1, keepdims=True))
    p_own = jnp.exp(lo - m)
    p_sel = jnp.exp(ls - m)
    denom = jnp.sum(p_own, axis=1, keepdims=True) + jnp.sum(p_sel, axis=1, keepdims=True)
    out = (jnp.dot(p_own.astype(_BF), v_own, preferred_element_type=_F32)
           + jnp.dot(p_sel.astype(_BF), vbuf[cur].astype(_BF), preferred_element_type=_F32))
    o_ref[...] = out / denom


def _moba_sample(qsrc, k_new, v_new, sel, page_table, rel_bias, k_pool, v_pool, batch, seq):
    n = batch * seq
    past = page_table.shape[1] * PAGE_SIZE
    assert past % MOBA_BLOCK == 0 and (past + seq - 1) // MOBA_BLOCK == past // MOBA_BLOCK
    rows = seq * MOBA_TOPK * MOBA_BLOCK
    grid_spec = pltpu.PrefetchScalarGridSpec(
        num_scalar_prefetch=2,
        grid=(batch, B_HEADS),
        in_specs=[
            pl.BlockSpec((seq, B_DH), lambda b, h, *_: (b, h)),
            pl.BlockSpec((seq, B_DH), lambda b, h, *_: (b, h)),
            pl.BlockSpec((seq, B_DH), lambda b, h, *_: (b, h)),
            pl.BlockSpec(memory_space=pltpu.SMEM),
            pl.BlockSpec(memory_space=pl.ANY),
            pl.BlockSpec(memory_space=pl.ANY),
        ],
        out_specs=pl.BlockSpec((seq, B_DH), lambda b, h, *_: (b, h)),
        scratch_shapes=[pltpu.VMEM((2, rows, B_DH), _F32), pltpu.VMEM((2, rows, B_DH), _F32),
                        pltpu.SemaphoreType.DMA((2, 2))],
    )
    return pl.pallas_call(
        functools.partial(_moba_sample_kernel, seq=seq, past=past),
        grid_spec=grid_spec,
        out_shape=jax.ShapeDtypeStruct((n, MIX_TOKEN), _F32),
        compiler_params=_params("arbitrary", "arbitrary"),
    )(sel, page_table, qsrc, k_new, v_new, rel_bias, k_pool, v_pool)


def _top_sorted(scores, count):
    tm = scores.shape[1]
    rank = lax.broadcasted_iota(jnp.int32, (count, tm), 0)
    vals = []
    stacked = jnp.zeros((count, tm), _F32)
    cur = scores
    for r in range(count):
        best = jnp.max(cur, axis=0, keepdims=True)
        vals.append(best)
        stacked = jnp.where(rank == r, best, stacked)
        cur = jnp.where(cur >= best, -jnp.inf, cur)
    return vals, stacked


def _peer_candidates(a, a_st, b, b_st):
    tm = a_st.shape[1]
    half = PEER_TOPK // 2
    groups = [a[0] + b_st[0:half], a[0] + b_st[half:PEER_TOPK]]
    groups += [a[i] + b_st[0:half] for i in range(1, half)]
    groups.append(a_st[half:PEER_TOPK] + b[0])
    return jnp.concatenate([jnp.broadcast_to(g, (half, tm)) for g in groups], axis=0)


def _peer_select_kernel(q_ref, keys_ref, s1_ref, s2_ref, e1_ref, e2_ref, tau_ref):
    tm = q_ref.shape[0]
    for h in range(PEER_HEADS):
        qh = q_ref[:, h * PEER_DKEY:(h + 1) * PEER_DKEY]
        st = lax.dot_general(keys_ref[h], qh, _NT, preferred_element_type=_F32,
                             precision=lax.Precision.HIGHEST)
        s1 = st[0:PEER_NKEYS]
        s2 = st[PEER_NKEYS:2 * PEER_NKEYS]
        a, a_st = _top_sorted(s1, PEER_TOPK)
        bb, b_st = _top_sorted(s2, PEER_TOPK)
        cand = _peer_candidates(a, a_st, bb, b_st)
        tau = jnp.zeros((1, tm), _F32)
        taken = jnp.zeros((1, tm), _F32)
        cur = cand
        for _ in range(PEER_TOPK):
            best = jnp.max(cur, axis=0, keepdims=True)
            hit = cur == best
            tau = jnp.where(taken < PEER_TOPK, best, tau)
            taken = taken + jnp.sum(jnp.where(hit, 1.0, 0.0), axis=0, keepdims=True)
            cur = jnp.where(hit, -jnp.inf, cur)
        top = a[0] + bb[0]
        z = jnp.sum(jnp.where(cand >= tau, jnp.exp(cand - top), 0.0), axis=0, keepdims=True)
        s1_ref[h] = s1
        s2_ref[h] = s2
        e1_ref[h] = jnp.exp(s1 - a[0]) / z
        e2_ref[h] = jnp.exp(s2 - bb[0])
        tau_ref[h] = jnp.broadcast_to(tau, (8, tm))


def _peer_select(q, keys_bd):
    n = q.shape[0]
    tm = min(256, n)
    half = pl.BlockSpec((PEER_HEADS, PEER_NKEYS, tm), lambda i: (0, 0, i))
    half_shape = jax.ShapeDtypeStruct((PEER_HEADS, PEER_NKEYS, n), _F32)
    return pl.pallas_call(
        _peer_select_kernel,
        grid=(n // tm,),
        in_specs=[pl.BlockSpec((tm, PEER_HEADS * PEER_DKEY), lambda i: (i, 0)),
                  pl.BlockSpec((PEER_HEADS, 2 * PEER_NKEYS, PEER_DKEY), lambda i: (0, 0, 0))],
        out_specs=[half, half, half, half, pl.BlockSpec((PEER_HEADS, 8, tm), lambda i: (0, 0, i))],
        out_shape=[half_shape, half_shape, half_shape, half_shape,
                   jax.ShapeDtypeStruct((PEER_HEADS, 8, n), _F32)],
        compiler_params=_params("parallel"),
    )(q, keys_bd)


PEER_TE = 1024
PEER_TM = 512
PEER_TM_GROUP = 512
PEER_SPLIT = 4
PEER_D_CHUNK = 2048


def _peer_dense_kernel(x_ref, u_ref, v_ref, s1_ref, s2_ref, e1_ref, e2_ref, tau_ref, o_ref, *, te, group):
    @pl.when(pl.program_id(1) == 0)
    def _():
        o_ref[...] = jnp.zeros_like(o_ref)

    tm = x_ref.shape[0]
    sub = te // PEER_SPLIT
    rows = sub // PEER_NKEYS
    for g0 in range(0, tm, group):
        cols = slice(g0, g0 + group)
        xg = x_ref[cols, :]
        a_t = [lax.dot_general(u_ref[s * sub:(s + 1) * sub, :], xg, _NT, preferred_element_type=_F32)
               for s in range(PEER_SPLIT)]
        for s in range(PEER_SPLIT):
            parts = []
            for r in range(rows):
                k1 = s * rows + r
                w = jnp.zeros((PEER_NKEYS, group), _F32)
                for h in range(PEER_HEADS):
                    total = s1_ref[h, k1:k1 + 1, cols] + s2_ref[h, :, cols]
                    w = w + jnp.where(total >= tau_ref[h, 0:1, cols],
                                      e2_ref[h, :, cols] * e1_ref[h, k1:k1 + 1, cols], 0.0)
                act = jax.nn.gelu(a_t[s][r * PEER_NKEYS:(r + 1) * PEER_NKEYS])
                parts.append((act * w).astype(_BF))
            gated = jnp.concatenate(parts, axis=0)
            for d0 in range(0, o_ref.shape[0], PEER_D_CHUNK):
                o_ref[d0:d0 + PEER_D_CHUNK, cols] += lax.dot_general(
                    v_ref[s * sub:(s + 1) * sub, d0:d0 + PEER_D_CHUNK], gated, _TN, preferred_element_type=_F32)


def _peer_dense(xb, u_stack, v_stack, layer, sel):
    n, d = xb.shape
    tm = min(PEER_TM, n)
    te = PEER_TE
    rows = te // PEER_NKEYS
    assert rows == 8
    held = pl.Buffered(1)
    per_key2 = pl.BlockSpec((PEER_HEADS, PEER_NKEYS, tm), lambda i, e: (0, 0, i), pipeline_mode=held)
    per_key1 = pl.BlockSpec((PEER_HEADS, rows, tm), lambda i, e: (0, e, i))
    s1, s2, e1, e2, tau = sel
    return pl.pallas_call(
        functools.partial(_peer_dense_kernel, te=te, group=min(PEER_TM_GROUP, tm)),
        grid=(n // tm, PEER_N // te),
        in_specs=[pl.BlockSpec((tm, d), lambda i, e: (i, 0), pipeline_mode=held),
                  pl.BlockSpec((None, te, d), lambda i, e: (layer, e, 0)),
                  pl.BlockSpec((None, te, d), lambda i, e: (layer, e, 0)),
                  per_key1, per_key2, per_key1, per_key2,
                  pl.BlockSpec((PEER_HEADS, 8, tm), lambda i, e: (0, 0, i), pipeline_mode=held)],
        out_specs=pl.BlockSpec((d, tm), lambda i, e: (0, i), pipeline_mode=held),
        out_shape=jax.ShapeDtypeStruct((d, n), _F32),
        compiler_params=_params("parallel", "arbitrary"),
    )(xb, u_stack, v_stack, s1, s2, e1, e2, tau)


def _peer(xb, layer, wq_stack, keys_bd, u_stack, v_stack):
    n = xb.shape[0]
    n_pad = max(n, 128)
    if n_pad != n:
        xb = jnp.pad(xb, ((0, n_pad - n), (0, 0)))
    q = _matmul([xb], [(wq_stack, layer, 0)], 1024, 512)
    return _peer_dense(xb, u_stack, v_stack, layer, _peer_select(q, keys_bd))


def _layer_tail(x, tok, mem, layer, w):
    n = x.shape[0]
    mixed = _matmul([tok, mem], [(w['out'], layer, 0), (w['out'], layer, MIX_TOKEN)], 512, 512)
    x1, x1b = _add_layernorm(x, mixed, *w['ln1'][layer])
    ffn_t = _peer(x1b, layer, w['peer_wq'], w['peer_keys'][layer], w['peer_u'], w['peer_v'])
    if ffn_t.shape[1] == n:
        return _add_layernorm(x1, ffn_t, *w['ln2'][layer], y_transposed=True)
    return _add_layernorm(x1, ffn_t[:, :n].T, *w['ln2'][layer])


def _trunk(x, batch, seq, mem_kv, mlstm_init, moba_attend, w):
    xb = x.astype(_BF)
    u = _matmul_nt(xb, w['in_a_main'], 1024, 512)
    qm = _matmul_nt(xb, w['in_a_mem'], 1024, 512)
    gates = _matmul_nt(x, w['in_a_gates'], 512, 128, precision=lax.Precision.HIGHEST)
    tok, state = _mlstm(u, gates[:, :2 * M_HEADS].T, w['b_if'], *mlstm_init, batch, seq)
    mem = _mem_attend(qm, 0, *mem_kv[0], batch, seq)
    x, xb = _layer_tail(x, tok, mem, 0, w)
    k_sh = _matmul([xb], [w['k_shared']], 1024, 512)
    v_sh = _matmul([xb], [w['v_shared']], 1024, 512)
    ub = _matmul([xb], [w['in_b']], 1024, 512)
    tok = moba_attend(ub, k_sh, v_sh)
    mem = _mem_attend(ub, MIX_TOKEN, *mem_kv[1], batch, seq)
    x, _ = _layer_tail(x, tok, mem, 1, w)
    return x, state, k_sh, v_sh


def kernel(x_prompt, x_sample, cache_moba_k, cache_moba_v, cache_mem_k, cache_mem_v, state_mlstm_c,
           state_mlstm_n, state_mlstm_m, page_table, mem_prompt, w_in_a, b_if_a, w_in_b, w_kv_shared, rel_bias,
           w_mem_kv, w_out, ln1_g, ln1_b, ln2_g, ln2_b, peer_wq, peer_keys, peer_u, peer_v):
    bp, tp, d = x_prompt.shape
    bs, ts, _ = x_sample.shape
    gate0 = 2 * M_HEADS * M_DQK + 2 * MIX_TOKEN
    wa = w_in_a[0]
    zeros = jnp.zeros((PEER_HEADS, PEER_NKEYS, PEER_DKEY // 2), _F32)

    def keys_blockdiag(keys):
        return jnp.concatenate([jnp.concatenate([keys[:, 0], zeros], axis=2),
                                jnp.concatenate([zeros, keys[:, 1]], axis=2)], axis=1)

    w = {
        'in_a_main': wa[:, :gate0].T.astype(_BF),
        'in_a_mem': wa[:, gate0 + 2 * M_HEADS:].T.astype(_BF),
        'in_a_gates': jnp.pad(wa[:, gate0:gate0 + 2 * M_HEADS].T, ((0, 128 - 2 * M_HEADS), (0, 0))),
        'b_if': b_if_a[0],
        'in_b': w_in_b[0].astype(_BF),
        'k_shared': w_kv_shared[:, :MIX_TOKEN].astype(_BF),
        'v_shared': w_kv_shared[:, MIX_TOKEN:].astype(_BF),
        'out': w_out.astype(_BF),
        'ln1': [(ln1_g[l], ln1_b[l]) for l in range(DEPTH)],
        'ln2': [(ln2_g[l], ln2_b[l]) for l in range(DEPTH)],
        'peer_wq': peer_wq.astype(_BF),
        'peer_keys': [keys_blockdiag(peer_keys[l]) for l in range(DEPTH)],
        'peer_u': peer_u.astype(_BF),
        'peer_v': peer_v.astype(_BF),
    }

    mem_w = jnp.concatenate([w_mem_kv[l] for l in range(DEPTH)], axis=1).astype(_BF)
    mkv = _matmul([mem_prompt.reshape(bp * MEM_SLOTS, d).astype(_BF)], [mem_w], 1024, 512)
    mkv3 = mkv.reshape(bp, MEM_SLOTS, 2 * DEPTH * MIX_MEM)
    mem_kv_p = [(mkv3, mkv3, 2 * l * MIX_MEM, (2 * l + 1) * MIX_MEM) for l in range(DEPTH)]
    init_p = (jnp.zeros((bp, M_HEADS, M_DQK, M_DV), _F32), jnp.zeros((bp, M_HEADS, M_DQK), _F32),
              jnp.zeros((bp, M_HEADS), _F32))
    bias = _bias_tiles(rel_bias, tp // MOBA_BLOCK)
    y_p, st_p, k_p, v_p = _trunk(
        x_prompt.reshape(bp * tp, d), bp, tp, mem_kv_p, init_p,
        lambda ub, k_sh, v_sh: _moba_prompt(ub, k_sh, v_sh, bias, bp, tp), w)

    mem_kv_s = [(cache_mem_k[l].reshape(bs, MEM_SLOTS, MIX_MEM), cache_mem_v[l].reshape(bs, MEM_SLOTS, MIX_MEM), 0, 0)
                for l in range(DEPTH)]
    init_s = (state_mlstm_c[0], state_mlstm_n[0], state_mlstm_m[0])
    nfull = page_table.shape[1] * PAGE_SIZE // MOBA_BLOCK

    def moba_sample(ub, k_sh, v_sh):
        means = _page_means(cache_moba_k, page_table, nfull)
        sel = _moba_sample_select(ub, means, bs, ts)
        return _moba_sample(ub, k_sh, v_sh, sel, page_table, rel_bias, cache_moba_k, cache_moba_v, bs, ts)

    y_s, st_s, k_s, v_s = _trunk(x_sample.reshape(bs * ts, d), bs, ts, mem_kv_s, init_s, moba_sample, w)

    mem_k_p = jnp.stack([mkv3[:, :, 2 * l * MIX_MEM:(2 * l + 1) * MIX_MEM] for l in range(DEPTH)])
    mem_v_p = jnp.stack([mkv3[:, :, (2 * l + 1) * MIX_MEM:(2 * l + 2) * MIX_MEM] for l in range(DEPTH)])
    kv_shape = (bp, MEM_SLOTS, MEM_HEADS, MEM_DH)
    return (y_p.reshape(bp, tp, d), y_s.reshape(bs, ts, d),
            st_p[0][None], st_p[1][None], st_p[2][None],
            k_p.reshape(bp, tp, B_HEADS, B_DH), v_p.reshape(bp, tp, B_HEADS, B_DH),
            mem_k_p.reshape((DEPTH,) + kv_shape), mem_v_p.reshape((DEPTH,) + kv_shape),
            st_s[0][None], st_s[1][None], st_s[2][None],
            k_s.reshape(bs, ts, B_HEADS, B_DH), v_s.reshape(bs, ts, B_HEADS, B_DH))
```

```python
import functools
import math

import jax
import jax.numpy as jnp
from jax import lax
from jax.experimental import pallas as pl
from jax.experimental.pallas import tpu as pltpu

D_MODEL = 4096
DEPTH = 2
PAGE_SIZE = 128
MIX_TOKEN = 3 * D_MODEL // 4
MIX_MEM = D_MODEL // 4
M_HEADS = 6
M_DV = MIX_TOKEN // M_HEADS
M_DQK = M_DV // 2
M_CHUNK_MAX = 256
M_CHUNK_MIN = 128
B_HEADS = 24
B_DH = MIX_TOKEN // B_HEADS
MOBA_BLOCK = 256
MOBA_TOPK = 3
MEM_SLOTS = 256
MEM_HEADS = 4
MEM_DH = MIX_MEM // MEM_HEADS
REL_BUCKETS = 32
REL_MAX_DIST = 4096
PEER_HEADS = 8
PEER_NKEYS = 128
PEER_N = PEER_NKEYS * PEER_NKEYS
PEER_DKEY = 128
PEER_TOPK = 16
DN_ALPHA = (2.0 * DEPTH) ** 0.25
LN_EPS = 1e-5

VMEM_LIMIT_V7X = 56 * 1024 * 1024
NEG_BIG = -1e30

_NT = (((1,), (1,)), ((), ()))
_TN = (((0,), (0,)), ((), ()))
_BF = jnp.bfloat16
_F32 = jnp.float32


def _params(*sem):
    return pltpu.CompilerParams(dimension_semantics=sem, vmem_limit_bytes=VMEM_LIMIT_V7X)


def _row_to_col(row, n):
    eye = lax.broadcasted_iota(jnp.int32, (n, n), 0) == lax.broadcasted_iota(jnp.int32, (n, n), 1)
    return jnp.sum(jnp.where(eye, jnp.broadcast_to(row, (n, n)), 0.0), axis=1, keepdims=True)


def _col_to_row(col, n):
    eye = lax.broadcasted_iota(jnp.int32, (n, n), 0) == lax.broadcasted_iota(jnp.int32, (n, n), 1)
    return jnp.sum(jnp.where(eye, jnp.broadcast_to(col, (n, n)), 0.0), axis=0, keepdims=True)


def _mm_kernel(*refs, n_pairs, precision):
    o_ref = refs[2 * n_pairs]
    acc = None
    for a_ref, w_ref in zip(refs[:n_pairs], refs[n_pairs:2 * n_pairs]):
        a = a_ref[...]
        if precision is None:
            a = a.astype(_BF)
        d = jnp.dot(a, w_ref[...], preferred_element_type=_F32, precision=precision)
        acc = d if acc is None else acc + d
    o_ref[...] = acc


def _weight_spec(w, k, tn):
    if not isinstance(w, tuple):
        return w, pl.BlockSpec((k, tn), lambda i, j: (0, j))
    stack, layer, row0 = w
    assert row0 % k == 0
    return stack, pl.BlockSpec((None, k, tn), lambda i, j: (layer, row0 // k, j))


def _matmul(a_list, w_list, tm, tn, precision=None):
    n = a_list[0].shape[0]
    w0 = w_list[0]
    n_out = (w0[0] if isinstance(w0, tuple) else w0).shape[-1]
    tm = min(tm, n)
    tn = min(tn, n_out)
    views = [_weight_spec(w, a.shape[1], tn) for a, w in zip(a_list, w_list)]
    in_specs = [pl.BlockSpec((tm, a.shape[1]), lambda i, j: (i, 0)) for a in a_list]
    in_specs += [spec for _, spec in views]
    return pl.pallas_call(
        functools.partial(_mm_kernel, n_pairs=len(a_list), precision=precision),
        grid=(n // tm, n_out // tn),
        in_specs=in_specs,
        out_specs=pl.BlockSpec((tm, tn), lambda i, j: (i, j)),
        out_shape=jax.ShapeDtypeStruct((n, n_out), _F32),
        compiler_params=_params("parallel", "parallel"),
    )(*a_list, *[arr for arr, _ in views])


def _mm_nt_kernel(a_ref, wt_ref, o_ref, *, precision):
    a = a_ref[...] if precision is not None else a_ref[...].astype(_BF)
    o_ref[...] = lax.dot_general(a, wt_ref[...], _NT, preferred_element_type=_F32, precision=precision)


def _matmul_nt(a, w_t, tm, tn, precision=None):
    n, k = a.shape
    n_out = w_t.shape[0]
    tm = min(tm, n)
    tn = min(tn, n_out)
    return pl.pallas_call(
        functools.partial(_mm_nt_kernel, precision=precision),
        grid=(n // tm, n_out // tn),
        in_specs=[pl.BlockSpec((tm, k), lambda i, j: (i, 0)), pl.BlockSpec((tn, k), lambda i, j: (j, 0))],
        out_specs=pl.BlockSpec((tm, tn), lambda i, j: (i, j)),
        out_shape=jax.ShapeDtypeStruct((n, n_out), _F32),
        compiler_params=_params("parallel", "parallel"),
    )(a, w_t)


def _ln_kernel(x_ref, y_ref, g_ref, b_ref, o_ref, ob_ref, *, y_transposed):
    y = y_ref[...].T if y_transposed else y_ref[...]
    z = DN_ALPHA * x_ref[...] + y
    mu = jnp.mean(z, axis=-1, keepdims=True)
    zc = z - mu
    var = jnp.mean(zc * zc, axis=-1, keepdims=True)
    y = zc * lax.rsqrt(var + LN_EPS) * g_ref[...] + b_ref[...]
    o_ref[...] = y
    ob_ref[...] = y.astype(_BF)


def _add_layernorm(x, y, g, b, y_transposed=False):
    n, d = x.shape
    tm = min(128, n)
    row = pl.BlockSpec((tm, d), lambda i: (i, 0))
    vec = pl.BlockSpec((1, d), lambda i: (0, 0))
    y_spec = pl.BlockSpec((d, tm), lambda i: (0, i)) if y_transposed else row
    return pl.pallas_call(
        functools.partial(_ln_kernel, y_transposed=y_transposed),
        grid=(n // tm,),
        in_specs=[row, y_spec, vec, vec],
        out_specs=[row, row],
        out_shape=[jax.ShapeDtypeStruct((n, d), _F32), jax.ShapeDtypeStruct((n, d), _BF)],
        compiler_params=_params("parallel"),
    )(x, y, g.reshape(1, d), b.reshape(1, d))


def _log_sigmoid(x):
    return jnp.minimum(x, 0.0) - jnp.log1p(jnp.exp(-jnp.abs(x)))


def _mlstm_kernel(bif_ref, q_ref, k_ref, v_ref, og_ref, gi_ref, gf_ref, c0_ref, n0_ref, m0_ref,
                  tok_ref, c_ref, n_ref, m_ref, *, l_in, l):
    h = pl.program_id(1)

    @pl.when(pl.program_id(2) == 0)
    def _():
        c_ref[...] = c0_ref[...]
        n_ref[...] = n0_ref[...]
        m_ref[...] = m0_ref[...]

    def rows(ref):
        x = ref[...]
        if l_in == l:
            return x
        return jnp.concatenate([x, jnp.zeros((l - l_in, x.shape[1]), x.dtype)], axis=0)

    qf = rows(q_ref)
    kf = rows(k_ref) * (M_DQK ** -0.5)
    vb = rows(v_ref).astype(_BF)
    i_row = gi_ref[0, 0] + bif_ref[h]
    f_row = gf_ref[0, 0] + bif_ref[M_HEADS + h]
    lf_row = _log_sigmoid(f_row)

    t_idx = lax.broadcasted_iota(jnp.int32, (l, l), 0)
    s_idx = lax.broadcasted_iota(jnp.int32, (l, l), 1)
    causal = s_idx <= t_idx
    b_col = jnp.sum(jnp.where(causal, jnp.broadcast_to(lf_row, (l, l)), 0.0), axis=1, keepdims=True)
    b_row = _col_to_row(b_col, l)
    i_col = _row_to_col(i_row, l)

    m_prev = m_ref[0, 0, :, 0:1]
    dmat = jnp.where(causal, b_col - b_row + i_row, -jnp.inf)
    inter = b_col + m_prev
    m_t = jnp.maximum(inter, jnp.max(dmat, axis=1, keepdims=True))
    w_inter = jnp.exp(inter - m_t)

    qb = qf.astype(_BF)
    kb = kf.astype(_BF)
    s = lax.dot_general(qb, kb, _NT, preferred_element_type=_F32) * jnp.exp(dmat - m_t)
    c_old = c_ref[0, 0]
    n_old = n_ref[0, 0]
    num = (jnp.dot(s.astype(_BF), vb, preferred_element_type=_F32)
           + w_inter * jnp.dot(qb, c_old.astype(_BF), preferred_element_type=_F32))
    den = jnp.sum(s, axis=1, keepdims=True) + w_inter * jnp.sum(qf * n_old, axis=1, keepdims=True)
    hid = num / jnp.maximum(jnp.abs(den), jnp.exp(-m_t))
    tok = jax.nn.sigmoid(rows(og_ref)) * hid
    tok_ref[...] = tok[0:l_in]

    m_new = m_t[l - 1:l, :]
    b_last = b_col[l - 1:l, :]
    w_end = jnp.exp(b_last - b_col + i_col - m_new)
    decay = jnp.exp(b_last + m_prev - m_new)
    kw = kf * w_end
    c_ref[0, 0] = decay * c_old + lax.dot_general(kw.astype(_BF), vb, _TN, preferred_element_type=_F32)
    n_ref[0, 0] = decay * n_old + jnp.sum(kw, axis=0, keepdims=True)
    m_ref[0, 0] = jnp.broadcast_to(m_new, (1, 128))


def _mlstm(u, gates_t, b_if, c0, n0, m0, batch, seq):
    n = batch * seq
    l = min(seq, M_CHUNK_MAX)
    nc = seq // l
    l_pad = max(l, M_CHUNK_MIN)
    gi = gates_t[:M_HEADS].reshape(M_HEADS, n // l, 1, l)
    gf = gates_t[M_HEADS:].reshape(M_HEADS, n // l, 1, l)
    if l_pad != l:
        gi = jnp.pad(gi, ((0, 0), (0, 0), (0, 0), (0, l_pad - l)), constant_values=NEG_BIG)
        gf = jnp.pad(gf, ((0, 0), (0, 0), (0, 0), (0, l_pad - l)), constant_values=-NEG_BIG)
    vq = M_DV // M_DQK
    row = lambda b, h, c: b * nc + c
    state_c = pl.BlockSpec((1, 1, M_DQK, M_DV), lambda b, h, c: (b, h, 0, 0))
    state_n = pl.BlockSpec((1, 1, 1, M_DQK), lambda b, h, c: (b, h, 0, 0))
    state_m = pl.BlockSpec((1, 1, 1, 128), lambda b, h, c: (b, h, 0, 0))
    gate = pl.BlockSpec((1, 1, 1, l_pad), lambda b, h, c: (h, row(b, h, c), 0, 0))
    tok, c_new, n_new, m_new = pl.pallas_call(
        functools.partial(_mlstm_kernel, l_in=l, l=l_pad),
        grid=(batch, M_HEADS, nc),
        in_specs=[
            pl.BlockSpec(memory_space=pltpu.SMEM),
            pl.BlockSpec((l, M_DQK), lambda b, h, c: (row(b, h, c), h)),
            pl.BlockSpec((l, M_DQK), lambda b, h, c: (row(b, h, c), M_HEADS + h)),
            pl.BlockSpec((l, M_DV), lambda b, h, c: (row(b, h, c), M_HEADS + h)),
            pl.BlockSpec((l, M_DV), lambda b, h, c: (row(b, h, c), 2 * M_HEADS + h)),
            gate, gate, state_c, state_n, state_m,
        ],
        out_specs=[pl.BlockSpec((l, M_DV), lambda b, h, c: (row(b, h, c), h)), state_c, state_n, state_m],
        out_shape=[
            jax.ShapeDtypeStruct((n, MIX_TOKEN), _F32),
            jax.ShapeDtypeStruct((batch, M_HEADS, M_DQK, M_DV), _F32),
            jax.ShapeDtypeStruct((batch, M_HEADS, 1, M_DQK), _F32),
            jax.ShapeDtypeStruct((batch, M_HEADS, 1, 128), _F32),
        ],
        compiler_params=_params("parallel", "parallel", "arbitrary"),
    )(b_if, u, u, u, u, gi, gf, c0, n0.reshape(batch, M_HEADS, 1, M_DQK),
      jnp.broadcast_to(m0[:, :, None, None], (batch, M_HEADS, 1, 128)))
    assert vq * M_DQK == M_DV
    return tok, (c_new, n_new.reshape(batch, M_HEADS, M_DQK), m_new[:, :, 0, 0])


def _memattn_kernel(q_ref, k_ref, v_ref, o_ref):
    q = q_ref[...].astype(_BF)
    logits = lax.dot_general(q, k_ref[0].astype(_BF), _NT, preferred_element_type=_F32) * (MEM_DH ** -0.5)
    e = jnp.exp(logits - jnp.max(logits, axis=-1, keepdims=True))
    p = e / jnp.sum(e, axis=-1, keepdims=True)
    o_ref[...] = jnp.dot(p.astype(_BF), v_ref[0].astype(_BF), preferred_element_type=_F32)


def _mem_attend(qsrc, q_col0, mk, mv, k_col0, v_col0, batch, seq):
    n = batch * seq
    tq = min(seq, 512)
    nq = seq // tq
    qc, kc, vc = q_col0 // MEM_DH, k_col0 // MEM_DH, v_col0 // MEM_DH
    return pl.pallas_call(
        _memattn_kernel,
        grid=(batch, nq, MEM_HEADS),
        in_specs=[
            pl.BlockSpec((tq, MEM_DH), lambda b, t, h: (b * nq + t, qc + h)),
            pl.BlockSpec((1, MEM_SLOTS, MEM_DH), lambda b, t, h: (b, 0, kc + h)),
            pl.BlockSpec((1, MEM_SLOTS, MEM_DH), lambda b, t, h: (b, 0, vc + h)),
        ],
        out_specs=pl.BlockSpec((tq, MEM_DH), lambda b, t, h: (b * nq + t, h)),
        out_shape=jax.ShapeDtypeStruct((n, MIX_MEM), _F32),
        compiler_params=_params("parallel", "parallel", "parallel"),
    )(qsrc, mk, mv)


def _rel_bucket(dist):
    n = jnp.maximum(dist, 0)
    max_exact = REL_BUCKETS // 2
    nf = jnp.maximum(n, 1).astype(_F32)
    large = max_exact + (jnp.log(nf / max_exact) / math.log(REL_MAX_DIST / max_exact)
                         * (REL_BUCKETS - max_exact)).astype(jnp.int32)
    return jnp.where(n < max_exact, n, jnp.minimum(large, REL_BUCKETS - 1))


def _bias_lookup(rb_ref, head, dist):
    bucket = _rel_bucket(dist)
    val = jnp.full(dist.shape, rb_ref[0, head], _F32)
    for b in range(1, REL_BUCKETS):
        val = jnp.where(bucket == b, rb_ref[b, head], val)
    return val


def _bias_tiles_kernel(rb_ref, o_ref):
    head, delta = pl.program_id(0), pl.program_id(1)
    key = lax.broadcasted_iota(jnp.int32, (MOBA_BLOCK, MOBA_BLOCK), 0)
    qry = lax.broadcasted_iota(jnp.int32, (MOBA_BLOCK, MOBA_BLOCK), 1)
    o_ref[0, 0] = _bias_lookup(rb_ref, head, delta * MOBA_BLOCK + qry - key)


def _bias_tiles(rel_bias, nb):
    return pl.pallas_call(
        _bias_tiles_kernel,
        grid=(B_HEADS, nb),
        in_specs=[pl.BlockSpec(memory_space=pltpu.SMEM)],
        out_specs=pl.BlockSpec((1, 1, MOBA_BLOCK, MOBA_BLOCK), lambda h, d: (h, d, 0, 0)),
        out_shape=jax.ShapeDtypeStruct((B_HEADS, nb, MOBA_BLOCK, MOBA_BLOCK), _F32),
        compiler_params=_params("parallel", "parallel"),
    )(rel_bias)


def _moba_prompt_kernel(q_ref, k_ref, v_ref, bias_ref, o_ref, s_scr, *, nb):
    blk = MOBA_BLOCK
    scale = B_DH ** -0.5
    means = jnp.concatenate(
        [jnp.mean(k_ref[c * blk:(c + 1) * blk, :], axis=0, keepdims=True) for c in range(nb)], axis=0)
    gate = lax.dot_general(means, q_ref[...], _NT, preferred_element_type=_F32,
                           precision=lax.Precision.HIGHEST)
    key = lax.broadcasted_iota(jnp.int32, (blk, blk), 0)
    qry = lax.broadcasted_iota(jnp.int32, (blk, blk), 1)
    causal_pen = jnp.where(key <= qry, 0.0, -jnp.inf)
    kb = [k_ref[c * blk:(c + 1) * blk, :].astype(_BF) for c in range(nb)]
    vb = [v_ref[c * blk:(c + 1) * blk, :].astype(_BF) for c in range(nb)]

    slot = 0
    for j in range(nb):
        qb = q_ref[j * blk:(j + 1) * blk, :].astype(_BF)
        g = [gate[c:c + 1, j * blk:(j + 1) * blk] for c in range(j)]
        pens = []
        for c in range(j):
            if j <= MOBA_TOPK:
                pens.append(None)
                continue
            rank = jnp.zeros((1, blk), _F32)
            for c2 in range(j):
                if c2 != c:
                    rank = rank + jnp.where(g[c2] >= g[c] if c2 < c else g[c2] > g[c], 1.0, 0.0)
            pens.append(jnp.where(rank < MOBA_TOPK, 0.0, -jnp.inf))

        m_run = jnp.full((1, blk), -jnp.inf, _F32)
        for c in range(j + 1):
            st = lax.dot_general(kb[c], qb, _NT, preferred_element_type=_F32) * scale + bias_ref[0, j - c]
            if c == j:
                st = st + causal_pen
            elif pens[c] is not None:
                st = st + pens[c]
            s_scr[slot + c] = st
            m_run = jnp.maximum(m_run, jnp.max(st, axis=0, keepdims=True))
        l_run = jnp.zeros((1, blk), _F32)
        acc = jnp.zeros((blk, B_DH), _F32)
        for c in range(j + 1):
            p = jnp.exp(s_scr[slot + c] - m_run)
            l_run = l_run + jnp.sum(p, axis=0, keepdims=True)
            acc = acc + lax.dot_general(p.astype(_BF), vb[c], _TN, preferred_element_type=_F32)
        o_ref[j * blk:(j + 1) * blk, :] = acc / _row_to_col(l_run, blk)
        slot += j + 1


def _moba_prompt(qsrc, k, v, bias, batch, seq):
    n = batch * seq
    nb = seq // MOBA_BLOCK
    per_head = pl.BlockSpec((seq, B_DH), lambda h, b: (b, h))
    return pl.pallas_call(
        functools.partial(_moba_prompt_kernel, nb=nb),
        grid=(B_HEADS, batch),
        in_specs=[per_head, per_head, per_head,
                  pl.BlockSpec((1, nb, MOBA_BLOCK, MOBA_BLOCK), lambda h, b: (h, 0, 0, 0))],
        out_specs=per_head,
        out_shape=jax.ShapeDtypeStruct((n, MIX_TOKEN), _F32),
        scratch_shapes=[pltpu.VMEM((nb * (nb + 1) // 2, MOBA_BLOCK, MOBA_BLOCK), _F32)],
        compiler_params=_params("parallel", "parallel"),
    )(qsrc, k, v, bias)


def _page_means_kernel(pt_ref, ka_ref, kb_ref, o_ref):
    o_ref[0, 0] = (jnp.sum(ka_ref[0], axis=0) + jnp.sum(kb_ref[0], axis=0)) / MOBA_BLOCK


def _page_means(k_pool, page_table, nfull):
    batch = page_table.shape[0]
    assert MOBA_BLOCK == 2 * PAGE_SIZE
    page = lambda which: pl.BlockSpec((1, PAGE_SIZE, B_HEADS, B_DH), lambda b, c, pt: (pt[b, 2 * c + which], 0, 0, 0))
    grid_spec = pltpu.PrefetchScalarGridSpec(
        num_scalar_prefetch=1,
        grid=(batch, nfull),
        in_specs=[page(0), page(1)],
        out_specs=pl.BlockSpec((1, 1, B_HEADS, B_DH), lambda b, c, pt: (b, c, 0, 0)),
    )
    return pl.pallas_call(
        _page_means_kernel,
        grid_spec=grid_spec,
        out_shape=jax.ShapeDtypeStruct((batch, nfull, B_HEADS, B_DH), _F32),
        compiler_params=_params("parallel", "parallel"),
    )(page_table, k_pool, k_pool)


def _moba_sample_select_kernel(q_ref, means_ref, sel_ref, *, seq, nfull):
    lanes = 128
    col = lax.broadcasted_iota(jnp.int32, (seq, lanes), 1)
    colf = col.astype(_F32)
    out = jnp.zeros((seq, lanes), _F32)
    for h in range(B_HEADS):
        qh = q_ref[:, h * B_DH:(h + 1) * B_DH]
        mh = means_ref[0, :, h, :]
        mh = jnp.concatenate([mh, jnp.zeros((lanes - nfull, B_DH), _F32)], axis=0)
        gate = lax.dot_general(qh, mh, _NT, preferred_element_type=_F32, precision=lax.Precision.HIGHEST)
        gate = jnp.where(col < nfull, gate, -jnp.inf)
        for kk in range(MOBA_TOPK):
            best = jnp.max(gate, axis=1, keepdims=True)
            idx = jnp.min(jnp.where(gate == best, colf, float(lanes)), axis=1, keepdims=True)
            out = jnp.where(col == h * MOBA_TOPK + kk, idx, out)
            gate = jnp.where(colf == idx, -jnp.inf, gate)
    sel_ref[0] = out.astype(jnp.int32)


def _moba_sample_select(qsrc, means, batch, seq):
    nfull = means.shape[1]
    assert nfull >= MOBA_TOPK and nfull <= 128 and B_HEADS * MOBA_TOPK <= 128
    return pl.pallas_call(
        functools.partial(_moba_sample_select_kernel, seq=seq, nfull=nfull),
        grid=(batch,),
        in_specs=[
            pl.BlockSpec((seq, MIX_TOKEN), lambda b: (b, 0)),
            pl.BlockSpec((1, nfull, B_HEADS, B_DH), lambda b: (b, 0, 0, 0)),
        ],
        out_specs=pl.BlockSpec((1, seq, 128), lambda b: (b, 0, 0)),
        out_shape=jax.ShapeDtypeStruct((batch, seq, 128), jnp.int32),
        compiler_params=_params("parallel"),
    )(qsrc, means)


def _moba_sample_kernel(sel_ref, pt_ref, q_ref, kn_ref, vn_ref, rb_ref, kpool_ref, vpool_ref, o_ref,
                        kbuf, vbuf, sems, *, seq, past):
    b, h = pl.program_id(0), pl.program_id(1)
    n_heads = pl.num_programs(1)
    step = b * n_heads + h
    cur = step % 2
    blk = MOBA_BLOCK
    ppb = blk // PAGE_SIZE
    nsel = seq * MOBA_TOPK

    def copies(bb, hh, half, t, kk, pg):
        block = sel_ref[bb, t, hh * MOBA_TOPK + kk]
        page = pt_ref[bb, block * ppb + pg]
        dst = pl.ds(((t * MOBA_TOPK + kk) * ppb + pg) * PAGE_SIZE, PAGE_SIZE)
        return (pltpu.make_async_copy(kpool_ref.at[page, :, hh, :], kbuf.at[half, dst, :], sems.at[0, half]),
                pltpu.make_async_copy(vpool_ref.at[page, :, hh, :], vbuf.at[half, dst, :], sems.at[1, half]))

    every = [(t, kk, pg) for t in range(seq) for kk in range(MOBA_TOPK) for pg in range(ppb)]

    def start_all(bb, hh, half):
        for i, idx in enumerate(every):
            for cp in copies(bb, hh, half, *idx):
                cp.start(priority=i % 2)

    @pl.when(step == 0)
    def _():
        start_all(b, h, cur)

    @pl.when(step + 1 < pl.num_programs(0) * n_heads)
    def _():
        wrap = h + 1 == n_heads
        start_all(jnp.where(wrap, b + 1, b), jnp.where(wrap, 0, h + 1), 1 - cur)

    qf = q_ref[...]
    qb = qf.astype(_BF)
    scale = B_DH ** -0.5
    pad = jnp.zeros((128 - seq, B_DH), _F32)
    k_own = jnp.concatenate([kn_ref[...], pad], axis=0).astype(_BF)
    v_own = jnp.concatenate([vn_ref[...], pad], axis=0).astype(_BF)
    t_own = lax.broadcasted_iota(jnp.int32, (seq, 128), 0)
    s_own = lax.broadcasted_iota(jnp.int32, (seq, 128), 1)
    lo = (lax.dot_general(qb, k_own, _NT, preferred_element_type=_F32) * scale
          + _bias_lookup(rb_ref, h, t_own - s_own))
    lo = jnp.where(s_own <= t_own, lo, -jnp.inf)

    ncol = nsel * blk
    t_sel = lax.broadcasted_iota(jnp.int32, (seq, ncol), 0)
    col = lax.broadcasted_iota(jnp.int32, (seq, ncol), 1)
    slot = col >> int(math.log2(blk))
    block_of_col = jnp.zeros((seq, ncol), jnp.int32)
    token_of_col = jnp.zeros((seq, ncol), jnp.int32)
    for t in range(seq):
        for kk in range(MOBA_TOPK):
            here = slot == t * MOBA_TOPK + kk
            block_of_col = jnp.where(here, sel_ref[b, t, h * MOBA_TOPK + kk], block_of_col)
            token_of_col = jnp.where(here, t, token_of_col)
    dist = past + t_sel - (block_of_col * blk + (col & (blk - 1)))
    bias_sel = _bias_lookup(rb_ref, h, dist)

    for idx in every:
        for cp in copies(b, h, cur, *idx):
            cp.wait()

    ls = lax.dot_general(qb, kbuf[cur].astype(_BF), _NT, preferred_element_type=_F32) * scale + bias_sel
    ls = jnp.where(token_of_col == t_sel, ls, -jnp.inf)
    m = jnp.maximum(jnp.max(lo, axis=1, keepdims=True), jnp.max(ls, axis=1, keepdims=True))
    p_own = jnp.exp(lo - m)
    p_sel = jnp.exp(ls - m)
    denom = jnp.sum(p_own, axis=1, keepdims=True) + jnp.sum(p_sel, axis=1, keepdims=True)
    out = (jnp.dot(p_own.astype(_BF), v_own, preferred_element_type=_F32)
           + jnp.dot(p_sel.astype(_BF), vbuf[cur].astype(_BF), preferred_element_type=_F32))
    o_ref[...] = out / denom


def _moba_sample(qsrc, k_new, v_new, sel, page_table, rel_bias, k_pool, v_pool, batch, seq):
    n = batch * seq
    past = page_table.shape[1] * PAGE_SIZE
    assert past % MOBA_BLOCK == 0 and (past + seq - 1) // MOBA_BLOCK == past // MOBA_BLOCK
    rows = seq * MOBA_TOPK * MOBA_BLOCK
    grid_spec = pltpu.PrefetchScalarGridSpec(
        num_scalar_prefetch=2,
        grid=(batch, B_HEADS),
        in_specs=[
            pl.BlockSpec((seq, B_DH), lambda b, h, *_: (b, h)),
            pl.BlockSpec((seq, B_DH), lambda b, h, *_: (b, h)),
            pl.BlockSpec((seq, B_DH), lambda b, h, *_: (b, h)),
            pl.BlockSpec(memory_space=pltpu.SMEM),
            pl.BlockSpec(memory_space=pl.ANY),
            pl.BlockSpec(memory_space=pl.ANY),
        ],
        out_specs=pl.BlockSpec((seq, B_DH), lambda b, h, *_: (b, h)),
        scratch_shapes=[pltpu.VMEM((2, rows, B_DH), _F32), pltpu.VMEM((2, rows, B_DH), _F32),
                        pltpu.SemaphoreType.DMA((2, 2))],
    )
    return pl.pallas_call(
        functools.partial(_moba_sample_kernel, seq=seq, past=past),
        grid_spec=grid_spec,
        out_shape=jax.ShapeDtypeStruct((n, MIX_TOKEN), _F32),
        compiler_params=_params("arbitrary", "arbitrary"),
    )(sel, page_table, qsrc, k_new, v_new, rel_bias, k_pool, v_pool)


def _top_sorted(scores, count):
    tm = scores.shape[1]
    rank = lax.broadcasted_iota(jnp.int32, (count, tm), 0)
    vals = []
    stacked = jnp.zeros((count, tm), _F32)
    cur = scores
    for r in range(count):
        best = jnp.max(cur, axis=0, keepdims=True)
        vals.append(best)
        stacked = jnp.where(rank == r, best, stacked)
        cur = jnp.where(cur >= best, -jnp.inf, cur)
    return vals, stacked


def _peer_candidates(a, a_st, b, b_st):
    tm = a_st.shape[1]
    half = PEER_TOPK // 2
    groups = [a[0] + b_st[0:half], a[0] + b_st[half:PEER_TOPK]]
    groups += [a[i] + b_st[0:half] for i in range(1, half)]
    groups.append(a_st[half:PEER_TOPK] + b[0])
    return jnp.concatenate([jnp.broadcast_to(g, (half, tm)) for g in groups], axis=0)


def _peer_select_kernel(q_ref, keys_ref, s1_ref, s2_ref, e1_ref, e2_ref, tau_ref):
    tm = q_ref.shape[0]
    for h in range(PEER_HEADS):
        qh = q_ref[:, h * PEER_DKEY:(h + 1) * PEER_DKEY]
        st = lax.dot_general(keys_ref[h], qh, _NT, preferred_element_type=_F32,
                             precision=lax.Precision.HIGHEST)
        s1 = st[0:PEER_NKEYS]
        s2 = st[PEER_NKEYS:2 * PEER_NKEYS]
        a, a_st = _top_sorted(s1, PEER_TOPK)
        bb, b_st = _top_sorted(s2, PEER_TOPK)
        cand = _peer_candidates(a, a_st, bb, b_st)
        tau = jnp.zeros((1, tm), _F32)
        taken = jnp.zeros((1, tm), _F32)
        cur = cand
        for _ in range(PEER_TOPK):
            best = jnp.max(cur, axis=0, keepdims=True)
            hit = cur == best
            tau = jnp.where(taken < PEER_TOPK, best, tau)
            taken = taken + jnp.sum(jnp.where(hit, 1.0, 0.0), axis=0, keepdims=True)
            cur = jnp.where(hit, -jnp.inf, cur)
        top = a[0] + bb[0]
        z = jnp.sum(jnp.where(cand >= tau, jnp.exp(cand - top), 0.0), axis=0, keepdims=True)
        s1_ref[h] = s1
        s2_ref[h] = s2
        e1_ref[h] = jnp.exp(s1 - a[0]) / z
        e2_ref[h] = jnp.exp(s2 - bb[0])
        tau_ref[h] = jnp.broadcast_to(tau, (8, tm))


def _peer_select(q, keys_bd):
    n = q.shape[0]
    tm = min(256, n)
    half = pl.BlockSpec((PEER_HEADS, PEER_NKEYS, tm), lambda i: (0, 0, i))
    half_shape = jax.ShapeDtypeStruct((PEER_HEADS, PEER_NKEYS, n), _F32)
    return pl.pallas_call(
        _peer_select_kernel,
        grid=(n // tm,),
        in_specs=[pl.BlockSpec((tm, PEER_HEADS * PEER_DKEY), lambda i: (i, 0)),
                  pl.BlockSpec((PEER_HEADS, 2 * PEER_NKEYS, PEER_DKEY), lambda i: (0, 0, 0))],
        out_specs=[half, half, half, half, pl.BlockSpec((PEER_HEADS, 8, tm), lambda i: (0, 0, i))],
        out_shape=[half_shape, half_shape, half_shape, half_shape,
                   jax.ShapeDtypeStruct((PEER_HEADS, 8, n), _F32)],
        compiler_params=_params("parallel"),
    )(q, keys_bd)


PEER_TE = 1024
PEER_TM = 512
PEER_TM_GROUP = 512
PEER_SPLIT = 4
PEER_D_CHUNK = 2048


def _peer_dense_kernel(x_ref, u_ref, v_ref, s1_ref, s2_ref, e1_ref, e2_ref, tau_ref, o_ref, *, te, group):
    @pl.when(pl.program_id(1) == 0)
    def _():
        o_ref[...] = jnp.zeros_like(o_ref)

    tm = x_ref.shape[0]
    sub = te // PEER_SPLIT
    rows = sub // PEER_NKEYS
    for g0 in range(0, tm, group):
        cols = slice(g0, g0 + group)
        xg = x_ref[cols, :]
        a_t = [lax.dot_general(u_ref[s * sub:(s + 1) * sub, :], xg, _NT, preferred_element_type=_F32)
               for s in range(PEER_SPLIT)]
        for s in range(PEER_SPLIT):
            parts = []
            for r in range(rows):
                k1 = s * rows + r
                w = jnp.zeros((PEER_NKEYS, group), _F32)
                for h in range(PEER_HEADS):
                    total = s1_ref[h, k1:k1 + 1, cols] + s2_ref[h, :, cols]
                    w = w + jnp.where(total >= tau_ref[h, 0:1, cols],
                                      e2_ref[h, :, cols] * e1_ref[h, k1:k1 + 1, cols], 0.0)
                act = jax.nn.gelu(a_t[s][r * PEER_NKEYS:(r + 1) * PEER_NKEYS])
                parts.append((act * w).astype(_BF))
            gated = jnp.concatenate(parts, axis=0)
            for d0 in range(0, o_ref.shape[0], PEER_D_CHUNK):
                o_ref[d0:d0 + PEER_D_CHUNK, cols] += lax.dot_general(
                    v_ref[s * sub:(s + 1) * sub, d0:d0 + PEER_D_CHUNK], gated, _TN, preferred_element_type=_F32)


def _peer_dense(xb, u_stack, v_stack, layer, sel):
    n, d = xb.shape
    tm = min(PEER_TM, n)
    te = PEER_TE
    rows = te // PEER_NKEYS
    assert rows == 8
    held = pl.Buffered(1)
    per_key2 = pl.BlockSpec((PEER_HEADS, PEER_NKEYS, tm), lambda i, e: (0, 0, i), pipeline_mode=held)
    per_key1 = pl.BlockSpec((PEER_HEADS, rows, tm), lambda i, e: (0, e, i))
    s1, s2, e1, e2, tau = sel
    return pl.pallas_call(
        functools.partial(_peer_dense_kernel, te=te, group=min(PEER_TM_GROUP, tm)),
        grid=(n // tm, PEER_N // te),
        in_specs=[pl.BlockSpec((tm, d), lambda i, e: (i, 0), pipeline_mode=held),
                  pl.BlockSpec((None, te, d), lambda i, e: (layer, e, 0)),
                  pl.BlockSpec((None, te, d), lambda i, e: (layer, e, 0)),
                  per_key1, per_key2, per_key1, per_key2,
                  pl.BlockSpec((PEER_HEADS, 8, tm), lambda i, e: (0, 0, i), pipeline_mode=held)],
        out_specs=pl.BlockSpec((d, tm), lambda i, e: (0, i), pipeline_mode=held),
        out_shape=jax.ShapeDtypeStruct((d, n), _F32),
        compiler_params=_params("parallel", "arbitrary"),
    )(xb, u_stack, v_stack, s1, s2, e1, e2, tau)


def _peer(xb, layer, wq_stack, keys_bd, u_stack, v_stack):
    n = xb.shape[0]
    n_pad = max(n, 128)
    if n_pad != n:
        xb = jnp.pad(xb, ((0, n_pad - n), (0, 0)))
    q = _matmul([xb], [(wq_stack, layer, 0)], 1024, 512)
    return _peer_dense(xb, u_stack, v_stack, layer, _peer_select(q, keys_bd))


def _layer_tail(x, tok, mem, layer, w):
    n = x.shape[0]
    mixed = _matmul([tok, mem], [(w['out'], layer, 0), (w['out'], layer, MIX_TOKEN)], 512, 512)
    x1, x1b = _add_layernorm(x, mixed, *w['ln1'][layer])
    ffn_t = _peer(x1b, layer, w['peer_wq'], w['peer_keys'][layer], w['peer_u'], w['peer_v'])
    if ffn_t.shape[1] == n:
        return _add_layernorm(x1, ffn_t, *w['ln2'][layer], y_transposed=True)
    return _add_layernorm(x1, ffn_t[:, :n].T, *w['ln2'][layer])


def _trunk(x, batch, seq, mem_kv, mlstm_init, moba_attend, w):
    xb = x.astype(_BF)
    u = _matmul_nt(xb, w['in_a_main'], 1024, 512)
    qm = _matmul_nt(xb, w['in_a_mem'], 1024, 512)
    gates = _matmul_nt(x, w['in_a_gates'], 512, 128, precision=lax.Precision.HIGHEST)
    tok, state = _mlstm(u, gates[:, :2 * M_HEADS].T, w['b_if'], *mlstm_init, batch, seq)
    mem = _mem_attend(qm, 0, *mem_kv[0], batch, seq)
    x, xb = _layer_tail(x, tok, mem, 0, w)
    k_sh = _matmul([xb], [w['k_shared']], 1024, 512)
    v_sh = _matmul([xb], [w['v_shared']], 1024, 512)
    ub = _matmul([xb], [w['in_b']], 1024, 512)
    tok = moba_attend(ub, k_sh, v_sh)
    mem = _mem_attend(ub, MIX_TOKEN, *mem_kv[1], batch, seq)
    x, _ = _layer_tail(x, tok, mem, 1, w)
    return x, state, k_sh, v_sh


def kernel(x_prompt, x_sample, cache_moba_k, cache_moba_v, cache_mem_k, cache_mem_v, state_mlstm_c,
           state_mlstm_n, state_mlstm_m, page_table, mem_prompt, w_in_a, b_if_a, w_in_b, w_kv_shared, rel_bias,
           w_mem_kv, w_out, ln1_g, ln1_b, ln2_g, ln2_b, peer_wq, peer_keys, peer_u, peer_v):
    bp, tp, d = x_prompt.shape
    bs, ts, _ = x_sample.shape
    gate0 = 2 * M_HEADS * M_DQK + 2 * MIX_TOKEN
    wa = w_in_a[0]
    zeros = jnp.zeros((PEER_HEADS, PEER_NKEYS, PEER_DKEY // 2), _F32)

    def keys_blockdiag(keys):
        return jnp.concatenate([jnp.concatenate([keys[:, 0], zeros], axis=2),
                                jnp.concatenate([zeros, keys[:, 1]], axis=2)], axis=1)

    w = {
        'in_a_main': wa[:, :gate0].T.astype(_BF),
        'in_a_mem': wa[:, gate0 + 2 * M_HEADS:].T.astype(_BF),
        'in_a_gates': jnp.pad(wa[:, gate0:gate0 + 2 * M_HEADS].T, ((0, 128 - 2 * M_HEADS), (0, 0))),
        'b_if': b_if_a[0],
        'in_b': w_in_b[0].astype(_BF),
        'k_shared': w_kv_shared[:, :MIX_TOKEN].astype(_BF),
        'v_shared': w_kv_shared[:, MIX_TOKEN:].astype(_BF),
        'out': w_out.astype(_BF),
        'ln1': [(ln1_g[l], ln1_b[l]) for l in range(DEPTH)],
        'ln2': [(ln2_g[l], ln2_b[l]) for l in range(DEPTH)],
        'peer_wq': peer_wq.astype(_BF),
        'peer_keys': [keys_blockdiag(peer_keys[l]) for l in range(DEPTH)],
        'peer_u': peer_u.astype(_BF),
        'peer_v': peer_v.astype(_BF),
    }

    mem_w = jnp.concatenate([w_mem_kv[l] for l in range(DEPTH)], axis=1).astype(_BF)
    mkv = _matmul([mem_prompt.reshape(bp * MEM_SLOTS, d).astype(_BF)], [mem_w], 1024, 512)
    mkv3 = mkv.reshape(bp, MEM_SLOTS, 2 * DEPTH * MIX_MEM)
    mem_kv_p = [(mkv3, mkv3, 2 * l * MIX_MEM, (2 * l + 1) * MIX_MEM) for l in range(DEPTH)]
    init_p = (jnp.zeros((bp, M_HEADS, M_DQK, M_DV), _F32), jnp.zeros((bp, M_HEADS, M_DQK), _F32),
              jnp.zeros((bp, M_HEADS), _F32))
    bias = _bias_tiles(rel_bias, tp // MOBA_BLOCK)
    y_p, st_p, k_p, v_p = _trunk(
        x_prompt.reshape(bp * tp, d), bp, tp, mem_kv_p, init_p,
        lambda ub, k_sh, v_sh: _moba_prompt(ub, k_sh, v_sh, bias, bp, tp), w)

    mem_kv_s = [(cache_mem_k[l].reshape(bs, MEM_SLOTS, MIX_MEM), cache_mem_v[l].reshape(bs, MEM_SLOTS, MIX_MEM), 0, 0)
                for l in range(DEPTH)]
    init_s = (state_mlstm_c[0], state_mlstm_n[0], state_mlstm_m[0])
    nfull = page_table.shape[1] * PAGE_SIZE // MOBA_BLOCK

    def moba_sample(ub, k_sh, v_sh):
        means = _page_means(cache_moba_k, page_table, nfull)
        sel = _moba_sample_select(ub, means, bs, ts)
        return _moba_sample(ub, k_sh, v_sh, sel, page_table, rel_bias, cache_moba_k, cache_moba_v, bs, ts)

    y_s, st_s, k_s, v_s = _trunk(x_sample.reshape(bs * ts, d), bs, ts, mem_kv_s, init_s, moba_sample, w)

    mem_k_p = jnp.stack([mkv3[:, :, 2 * l * MIX_MEM:(2 * l + 1) * MIX_MEM] for l in range(DEPTH)])
    mem_v_p = jnp.stack([mkv3[:, :, (2 * l + 1) * MIX_MEM:(2 * l + 2) * MIX_MEM] for l in range(DEPTH)])
    kv_shape = (bp, MEM_SLOTS, MEM_HEADS, MEM_DH)
    return (y_p.reshape(bp, tp, d), y_s.reshape(bs, ts, d),
            st_p[0][None], st_p[1][None], st_p[2][None],
            k_p.reshape(bp, tp, B_HEADS, B_DH), v_p.reshape(bp, tp, B_HEADS, B_DH),
            mem_k_p.reshape((DEPTH,) + kv_shape), mem_v_p.reshape((DEPTH,) + kv_shape),
            st_s[0][None], st_s[1][None], st_s[2][None],
            k_s.reshape(bs, ts, B_HEADS, B_DH), v_s.reshape(bs, ts, B_HEADS, B_DH))
```
